```python
import jax, jax.numpy as jnp
from jax import lax
import numpy as np

D_MODEL = 2048
BATCH = 2
SEQ = 4096
DEPTH = 2

N_HEADS = 16
Q_LORA = 512
KV_LORA = 256
QK_NOPE = 128
QK_ROPE = 64
V_HEAD = 128
QK_HEAD = QK_NOPE + QK_ROPE
ATTN_WIDTH = N_HEADS * V_HEAD
ROPE_THETA = 10000.0
Q_BLOCK = 128
CONV_WIDTH = 1024
CONV_K = 3
N_EXPERTS = 64
TOP_K = 8
N_GROUPS = 8
TOPK_GROUPS = 4
EXPERTS_PER_GROUP = N_EXPERTS // N_GROUPS
D_EXPERT = 512
D_SHARED = 512
ROUTED_SCALE = 2.5
ROW_BLOCK = 128
N_MOD = 6
EPS = 1e-6
IN_SIZES = (Q_LORA, KV_LORA, QK_ROPE, CONV_WIDTH, CONV_WIDTH, CONV_WIDTH, D_MODEL, D_MODEL)
N_IN = Q_LORA + KV_LORA + QK_ROPE + 3 * CONV_WIDTH + 2 * D_MODEL

kernel_name = 'hybrid_mla_shortconv_moe_adaln_encoder'


def rmsnorm(x, g):
    xf = x.astype(jnp.float32)
    y = xf * lax.rsqrt(jnp.mean(xf * xf, axis=-1, keepdims=True) + EPS)
    return (y * g.astype(jnp.float32)).astype(x.dtype)


def modulate(h, shift, scale):
    return h * (1 + scale[:, None, :]) + shift[:, None, :]


def rope_tables(positions, dtype):
    inv = 1.0 / (ROPE_THETA ** (jnp.arange(0, QK_ROPE, 2, dtype=jnp.float32) / QK_ROPE))
    ang = positions.astype(jnp.float32)[..., None] * inv
    return jnp.cos(ang).astype(dtype), jnp.sin(ang).astype(dtype)


def apply_rope(t, cos, sin):
    t1, t2 = jnp.split(t, 2, axis=-1)
    return jnp.concatenate([t1 * cos - t2 * sin, t2 * cos + t1 * sin], axis=-1)


def mla_attention(c_q, c_kv, k_pe_raw, cos, sin, g_q, w_uq, g_kv, w_ukv, w_o):
    b, s, _ = c_q.shape
    nb = s // Q_BLOCK
    q = jnp.einsum('bsr,rhd->bshd', rmsnorm(c_q, g_q), w_uq) * (QK_HEAD ** -0.5)
    q_nope = q[..., :QK_NOPE]
    q_pe = apply_rope(q[..., QK_NOPE:], cos[:, :, None, :], sin[:, :, None, :])
    kv = jnp.einsum('bsr,rhd->bshd', rmsnorm(c_kv, g_kv), w_ukv)
    k_nope, v = kv[..., :QK_NOPE], kv[..., QK_NOPE:]
    k_pe = apply_rope(k_pe_raw, cos, sin)

    def to_blocks(t):
        return t.reshape(b, nb, Q_BLOCK, *t.shape[2:]).swapaxes(0, 1)

    def attend(blk):
        qn, qp = blk
        sc = (jnp.einsum('bqhd,bkhd->bhqk', qn, k_nope)
              + jnp.einsum('bqhr,bkr->bhqk', qp, k_pe))
        p = jax.nn.softmax(sc.astype(jnp.float32), axis=-1).astype(v.dtype)
        return jnp.einsum('bhqk,bkhd->bqhd', p, v)

    o = lax.map(attend, (to_blocks(q_nope), to_blocks(q_pe)))
    o = o.swapaxes(0, 1).reshape(b, s, ATTN_WIDTH)
    return o @ w_o


def short_conv(gate_b, gate_c, u, conv_w, w_conv_out):
    z = gate_c * u
    z = lax.conv_general_dilated(z, conv_w, window_strides=(1,), padding=((1, 1),),
                                 dimension_numbers=('NWC', 'WIO', 'NWC'),
                                 feature_group_count=CONV_WIDTH)
    return (gate_b * z) @ w_conv_out


def swiglu(h, w1, w3, w2):
    return (jax.nn.silu(h @ w1) * (h @ w3)) @ w2


def route(h, w_router, b_router):
    t = h.shape[0]
    scores = jax.nn.sigmoid(h.astype(jnp.float32) @ w_router.astype(jnp.float32))
    biased = scores + b_router.astype(jnp.float32)
    grp_score = lax.top_k(biased.reshape(t, N_GROUPS, EXPERTS_PER_GROUP), 2)[0].sum(-1)
    _, grp_idx = lax.top_k(grp_score, TOPK_GROUPS)
    grp_mask = jnp.any(grp_idx[:, :, None] == jnp.arange(N_GROUPS)[None, None, :], axis=1)
    allowed = jnp.repeat(grp_mask, EXPERTS_PER_GROUP, axis=1)
    _, idx = lax.top_k(jnp.where(allowed, biased, -jnp.inf), TOP_K)
    wts = jnp.take_along_axis(scores, idx, axis=-1)
    wts = wts / jnp.sum(wts, axis=-1, keepdims=True) * ROUTED_SCALE
    return idx, wts


def routed_experts(h, idx, wts, w1, w3, w2):
    t, d = h.shape
    n_assign = t * TOP_K
    n_rows = n_assign + N_EXPERTS * ROW_BLOCK
    n_blk = n_rows // ROW_BLOCK
    flat_e = idx.reshape(-1)
    flat_t = jnp.repeat(jnp.arange(t, dtype=jnp.int32), TOP_K)
    flat_w = wts.reshape(-1)
    order = jnp.argsort(flat_e)
    se = flat_e[order]
    counts = jnp.bincount(flat_e, length=N_EXPERTS)
    start = jnp.cumsum(counts) - counts
    padded = (counts + ROW_BLOCK - 1) // ROW_BLOCK * ROW_BLOCK
    pend = jnp.cumsum(padded)
    pstart = pend - padded
    dest = pstart[se] + jnp.arange(n_assign, dtype=jnp.int32) - start[se]
    row_tok = jnp.full((n_rows,), t, jnp.int32).at[dest].set(flat_t[order])
    row_w = jnp.zeros((n_rows,), h.dtype).at[dest].set(flat_w[order].astype(h.dtype))
    blk_start = jnp.arange(n_blk, dtype=jnp.int32) * ROW_BLOCK
    blk_e = jnp.minimum(jnp.searchsorted(pend, blk_start, side='right'), N_EXPERTS - 1)
    h_pad = jnp.concatenate([h, jnp.zeros((1, d), h.dtype)], axis=0)

    def run(blk):
        e, tok = blk
        xb = h_pad[tok]
        return (jax.nn.silu(xb @ w1[e]) * (xb @ w3[e])) @ w2[e]

    out = lax.map(run, (blk_e, row_tok.reshape(n_blk, ROW_BLOCK)))
    out = out.reshape(n_rows, d) * row_w[:, None]
    return jnp.zeros((t + 1, d), h.dtype).at[row_tok].add(out)[:t]


def setup_inputs(seed: int = 0) -> dict:
    key = jax.random.key(seed)
    ks = jax.random.split(key, 25)
    L, D = DEPTH, D_MODEL

    def w(k, shape, fan_in, gain=1.0):
        return jax.random.normal(k, shape, jnp.float32) * (gain * fan_in ** -0.5)

    def g(k, shape):
        return 1.0 + 0.02 * jax.random.normal(k, shape, jnp.float32)

    offset = jax.random.randint(ks[2], (BATCH, 1), 0, 1024, dtype=jnp.int32)
    positions = offset + jnp.arange(SEQ, dtype=jnp.int32)[None, :]
    return {
        'x': jax.random.normal(ks[0], (BATCH, SEQ, D), jnp.float32),
        'c': jax.random.normal(ks[1], (BATCH, D), jnp.float32),
        'positions': positions,
        'w_ada': w(ks[3], (L, D, N_MOD * D), D, 0.5),
        'b_ada': 0.02 * jax.random.normal(ks[4], (L, N_MOD * D), jnp.float32),
        'g_mix': g(ks[5], (L, D)),
        'w_in': w(ks[6], (L, D, N_IN), D),
        'g_q': g(ks[7], (L, Q_LORA)),
        'w_uq': w(ks[8], (L, Q_LORA, N_HEADS, QK_HEAD), Q_LORA),
        'g_kv': g(ks[9], (L, KV_LORA)),
        'w_ukv': w(ks[10], (L, KV_LORA, N_HEADS, QK_NOPE + V_HEAD), KV_LORA),
        'w_o_attn': w(ks[11], (L, ATTN_WIDTH, D), ATTN_WIDTH),
        'conv_w': w(ks[12], (L, CONV_K, 1, CONV_WIDTH), CONV_K),
        'w_conv_out': w(ks[13], (L, CONV_WIDTH, D), CONV_WIDTH),
        'w_out': w(ks[14], (L, D, D), D),
        'g_ffn': g(ks[15], (L, D)),
        'w_router': w(ks[16], (L, D, N_EXPERTS), D),
        'b_router': 0.01 * jax.random.normal(ks[17], (L, N_EXPERTS), jnp.float32),
        'w1_e': w(ks[18], (L, N_EXPERTS, D, D_EXPERT), D),
        'w3_e': w(ks[19], (L, N_EXPERTS, D, D_EXPERT), D),
        'w2_e': w(ks[20], (L, N_EXPERTS, D_EXPERT, D), D_EXPERT),
        'w1_s': w(ks[21], (L, D, D_SHARED), D),
        'w3_s': w(ks[22], (L, D, D_SHARED), D),
        'w2_s': w(ks[23], (L, D_SHARED, D), D_SHARED),
        'g_final': g(ks[24], (D,)),
    }


def reference(x, c, positions, w_ada, b_ada, g_mix, w_in, g_q, w_uq, g_kv, w_ukv,
              w_o_attn, conv_w, w_conv_out, w_out, g_ffn, w_router, b_router,
              w1_e, w3_e, w2_e, w1_s, w3_s, w2_s, g_final):
    b, s, d = x.shape
    cos, sin = rope_tables(positions, x.dtype)
    c_act = jax.nn.silu(c)
    split_at = np.cumsum(IN_SIZES)[:-1].tolist()
    for i in range(DEPTH):
        mod = c_act @ w_ada[i] + b_ada[i]
        sh_m, sc_m, gt_m, sh_f, sc_f, gt_f = jnp.split(mod, N_MOD, axis=-1)

        h = modulate(rmsnorm(x, g_mix[i]), sh_m, sc_m)
        (c_q, c_kv, k_pe_raw, gate_b, gate_c, u,
         logit_a, logit_c) = jnp.split(h @ w_in[i], split_at, axis=-1)
        y_attn = mla_attention(c_q, c_kv, k_pe_raw, cos, sin, g_q[i], w_uq[i],
                               g_kv[i], w_ukv[i], w_o_attn[i])
        y_conv = short_conv(gate_b, gate_c, u, conv_w[i], w_conv_out[i])
        y = (jax.nn.sigmoid(logit_a) * y_attn + jax.nn.sigmoid(logit_c) * y_conv) @ w_out[i]
        x = x + gt_m[:, None, :] * y

        h = modulate(rmsnorm(x, g_ffn[i]), sh_f, sc_f).reshape(b * s, d)
        idx, wts = route(h, w_router[i], b_router[i])
        y = (routed_experts(h, idx, wts, w1_e[i], w3_e[i], w2_e[i])
             + swiglu(h, w1_s[i], w3_s[i], w2_s[i]))
        x = x + gt_f[:, None, :] * y.reshape(b, s, d)
    return rmsnorm(x, g_final)
```

```python
import functools

import jax
import jax.numpy as jnp
from jax import lax
from jax.experimental import pallas as pl
from jax.experimental.pallas import tpu as pltpu

F32 = jnp.float32
BF16 = jnp.bfloat16

N_HEADS = 16
Q_LORA = 512
KV_LORA = 256
QK_NOPE = 128
QK_ROPE = 64
V_HEAD = 128
QK_HEAD = QK_NOPE + QK_ROPE
ROPE_THETA = 10000.0
CONV_WIDTH = 1024
N_EXPERTS = 64
TOP_K = 8
N_GROUPS = 8
TOPK_GROUPS = 4
EXPERTS_PER_GROUP = N_EXPERTS // N_GROUPS
ROUTED_SCALE = 2.5
N_MOD = 6
EPS = 1e-6

LANES = 128
BF16_SUBLANES = 16
HEAD_PAD = 2 * LANES

COL_LA = 0
COL_LC = 2048
COL_GB = 4096
COL_GC = 5120
COL_U = 6144
COL_CQ = 7168
COL_CKV = 7680
COL_KPE = 7936
N_PROJ = 8064

VMEM_LIMIT = 56 * 1024 * 1024

TM_IN = 512
TN_IN = 1152
TM_QKV = 512
TQ_ATTN = 512
TM_MERGE = 256
TM_ROUTE = 512
TM_EXP = 256
TM_COMB = 256
TN_ADA = 1024


def _cparams(sem):
    return pltpu.CompilerParams(dimension_semantics=sem, vmem_limit_bytes=VMEM_LIMIT)


def _rms(x):
    return x * lax.rsqrt(jnp.mean(x * x, axis=-1, keepdims=True) + EPS)


def _silu(x):
    return x * jax.nn.sigmoid(x)


def _ada_kernel(c_ref, w_ref, b_ref, o_ref):
    ca = _silu(c_ref[...]).astype(BF16)
    o_ref[0] = jnp.dot(ca, w_ref[0].astype(BF16), preferred_element_type=F32) + b_ref[0]


def _ada_call(c8, w_ada, b_ada3):
    depth, d, n = w_ada.shape
    return pl.pallas_call(
        _ada_kernel,
        grid=(depth, n // TN_ADA),
        in_specs=[
            pl.BlockSpec((8, d), lambda l, j: (0, 0)),
            pl.BlockSpec((1, d, TN_ADA), lambda l, j: (l, 0, j)),
            pl.BlockSpec((1, 1, TN_ADA), lambda l, j: (l, 0, j)),
        ],
        out_specs=pl.BlockSpec((1, 8, TN_ADA), lambda l, j: (l, 0, j)),
        out_shape=jax.ShapeDtypeStruct((depth, 8, n), F32),
        compiler_params=_cparams(("parallel", "parallel")),
        name="ada_mod",
    )(c8, w_ada, b_ada3)


def _rope_kernel(pos_ref, inv_ref, cos_ref, sin_ref):
    ang = pos_ref[...].astype(F32) * inv_ref[...]
    lane = lax.broadcasted_iota(jnp.int32, ang.shape, 1)
    half = QK_ROPE // 2
    cos_ref[...] = jnp.where(lane < QK_ROPE, jnp.cos(ang), 0.0)
    s = jnp.sin(ang)
    sin_ref[...] = jnp.where(lane < half, -s, jnp.where(lane < QK_ROPE, s, 0.0))


def _rope_call(pos_col, inv128):
    t = pos_col.shape[0]
    tm = 1024
    return pl.pallas_call(
        _rope_kernel,
        grid=(t // tm,),
        in_specs=[pl.BlockSpec((tm, 1), lambda i: (i, 0)),
                  pl.BlockSpec((1, LANES), lambda i: (0, 0))],
        out_specs=[pl.BlockSpec((tm, LANES), lambda i: (i, 0)),
                   pl.BlockSpec((tm, LANES), lambda i: (i, 0))],
        out_shape=[jax.ShapeDtypeStruct((t, LANES), F32)] * 2,
        compiler_params=_cparams(("parallel",)),
        name="rope_tables",
    )(pos_col, inv128)


def _in_kernel(x_ref, g_ref, mod_ref, w_ref, o_ref, h_ref):
    @pl.when(pl.program_id(1) == 0)
    def _():
        m = mod_ref[0]
        h = _rms(x_ref[...]) * g_ref[...]
        h_ref[...] = (h * (1.0 + m[1:2]) + m[0:1]).astype(BF16)

    o_ref[...] = jnp.dot(h_ref[...], w_ref[...], preferred_element_type=F32).astype(o_ref.dtype)


def _in_call(x, g, mod, w_in_r, seq):
    t, d = x.shape
    tiles_per_seq = seq // TM_IN
    return pl.pallas_call(
        _in_kernel,
        grid=(t // TM_IN, N_PROJ // TN_IN),
        in_specs=[
            pl.BlockSpec((TM_IN, d), lambda i, j: (i, 0)),
            pl.BlockSpec((1, d), lambda i, j: (0, 0)),
            pl.BlockSpec((1, N_MOD, d), lambda i, j: (i // tiles_per_seq, 0, 0)),
            pl.BlockSpec((d, TN_IN), lambda i, j: (0, j)),
        ],
        out_specs=pl.BlockSpec((TM_IN, TN_IN), lambda i, j: (i, j)),
        out_shape=jax.ShapeDtypeStruct((t, N_PROJ), BF16),
        scratch_shapes=[pltpu.VMEM((TM_IN, d), BF16)],
        compiler_params=_cparams(("parallel", "arbitrary")),
        name="in_proj",
    )(x, g, mod, w_in_r)


def _rope128(t, cos, sin_signed):
    lane = lax.broadcasted_iota(jnp.int32, t.shape, 1)
    half = QK_ROPE // 2
    partner = jnp.where(lane < half, pltpu.roll(t, LANES - half, 1), pltpu.roll(t, half, 1))
    return t * cos + partner * sin_signed


def _qkv_kernel(cq_ref, ckv_ref, kpe_ref, cos_ref, sin_ref, gq_ref, gkv_ref,
                wq_ref, wk_ref, wv_ref, q_ref, k_ref, v_ref):
    cos = cos_ref[...]
    sin = sin_ref[...]
    qn = (_rms(cq_ref[...].astype(F32)) * gq_ref[...]).astype(BF16)
    q = jnp.dot(qn, wq_ref[...], preferred_element_type=F32) * (QK_HEAD ** -0.5)
    cn = (_rms(ckv_ref[...].astype(F32)) * gkv_ref[...]).astype(BF16)
    kn = jnp.dot(cn, wk_ref[...], preferred_element_type=F32)
    v_ref[...] = jnp.dot(cn, wv_ref[...], preferred_element_type=F32).astype(BF16)
    kr = _rope128(kpe_ref[...].astype(F32), cos, sin).astype(BF16)
    for h in range(N_HEADS):
        base = h * HEAD_PAD
        q_ref[:, base:base + LANES] = q[:, base:base + LANES].astype(BF16)
        q_ref[:, base + LANES:base + HEAD_PAD] = _rope128(
            q[:, base + LANES:base + HEAD_PAD], cos, sin).astype(BF16)
        k_ref[:, base:base + LANES] = kn[:, h * QK_NOPE:(h + 1) * QK_NOPE].astype(BF16)
        k_ref[:, base + LANES:base + HEAD_PAD] = kr


def _qkv_call(proj, cos, sin, gq, gkv, wq, wk, wv):
    t = proj.shape[0]
    tm = TM_QKV
    return pl.pallas_call(
        _qkv_kernel,
        grid=(t // tm,),
        in_specs=[
            pl.BlockSpec((tm, Q_LORA), lambda i: (i, COL_CQ // Q_LORA)),
            pl.BlockSpec((tm, KV_LORA), lambda i: (i, COL_CKV // KV_LORA)),
            pl.BlockSpec((tm, LANES), lambda i: (i, COL_KPE // LANES)),
            pl.BlockSpec((tm, LANES), lambda i: (i, 0)),
            pl.BlockSpec((tm, LANES), lambda i: (i, 0)),
            pl.BlockSpec((1, Q_LORA), lambda i: (0, 0)),
            pl.BlockSpec((1, KV_LORA), lambda i: (0, 0)),
            pl.BlockSpec(wq.shape, lambda i: (0, 0)),
            pl.BlockSpec(wk.shape, lambda i: (0, 0)),
            pl.BlockSpec(wv.shape, lambda i: (0, 0)),
        ],
        out_specs=[
            pl.BlockSpec((tm, N_HEADS * HEAD_PAD), lambda i: (i, 0)),
            pl.BlockSpec((tm, N_HEADS * HEAD_PAD), lambda i: (i, 0)),
            pl.BlockSpec((tm, N_HEADS * V_HEAD), lambda i: (i, 0)),
        ],
        out_shape=[
            jax.ShapeDtypeStruct((t, N_HEADS * HEAD_PAD), BF16),
            jax.ShapeDtypeStruct((t, N_HEADS * HEAD_PAD), BF16),
            jax.ShapeDtypeStruct((t, N_HEADS * V_HEAD), BF16),
        ],
        compiler_params=_cparams(("parallel",)),
        name="qkv_build",
    )(proj, proj, proj, cos, sin, gq, gkv, wq, wk, wv)


def _attn_kernel(q_ref, k_ref, v_ref, o_ref):
    s = lax.dot_general(q_ref[...], k_ref[...], (((1,), (1,)), ((), ())),
                        preferred_element_type=F32)
    m = jnp.max(s, axis=-1, keepdims=True)
    p = jnp.exp(s - m)
    l = jnp.sum(p, axis=-1, keepdims=True)
    o = jnp.dot(p.astype(BF16), v_ref[...], preferred_element_type=F32)
    o_ref[...] = (o / l).astype(o_ref.dtype)


def _attn_call(q, k, v, batch, seq):
    t = q.shape[0]
    nq = seq // TQ_ATTN
    return pl.pallas_call(
        _attn_kernel,
        grid=(batch, N_HEADS, nq),
        in_specs=[
            pl.BlockSpec((TQ_ATTN, HEAD_PAD), lambda b, h, i: (b * nq + i, h)),
            pl.BlockSpec((seq, HEAD_PAD), lambda b, h, i: (b, h)),
            pl.BlockSpec((seq, V_HEAD), lambda b, h, i: (b, h)),
        ],
        out_specs=pl.BlockSpec((TQ_ATTN, V_HEAD), lambda b, h, i: (b * nq + i, h)),
        out_shape=jax.ShapeDtypeStruct((t, N_HEADS * V_HEAD), BF16),
        compiler_params=_cparams(("parallel", "parallel", "arbitrary")),
        name="mla_attention",
    )(q, k, v)


def _merge_kernel(o_ref, la_ref, lc_ref, gb_ref, gc_ref, u_ref, gcp_ref, up_ref, gcn_ref, un_ref,
                  x_ref, mod_ref, cw_ref, wo_ref, wco_ref, wout_ref, out_ref, *, tiles_per_seq):
    i = pl.program_id(0)
    tm = x_ref.shape[0]
    pos = i % tiles_per_seq
    has_prev = (pos != 0).astype(F32)
    has_next = (pos != tiles_per_seq - 1).astype(F32)
    z = gc_ref[...].astype(F32) * u_ref[...].astype(F32)
    last = BF16_SUBLANES - 1
    z_before = gcp_ref[last:last + 1, :].astype(F32) * up_ref[last:last + 1, :].astype(F32) * has_prev
    z_after = gcn_ref[0:1, :].astype(F32) * un_ref[0:1, :].astype(F32) * has_next
    row = lax.broadcasted_iota(jnp.int32, z.shape, 0)
    z_prev = jnp.where(row == 0, z_before, pltpu.roll(z, 1, 0))
    z_next = jnp.where(row == tm - 1, z_after, pltpu.roll(z, tm - 1, 0))
    cw = cw_ref[...]
    conv = cw[0:1] * z_prev + cw[1:2] * z + cw[2:3] * z_next
    y_conv = jnp.dot((gb_ref[...].astype(F32) * conv).astype(BF16), wco_ref[...],
                     preferred_element_type=F32)
    y_attn = jnp.dot(o_ref[...], wo_ref[...], preferred_element_type=F32)
    mix = (jax.nn.sigmoid(la_ref[...].astype(F32)) * y_attn
           + jax.nn.sigmoid(lc_ref[...].astype(F32)) * y_conv)
    y = jnp.dot(mix.astype(BF16), wout_ref[...], preferred_element_type=F32)
    out_ref[...] = x_ref[...] + mod_ref[0][2:3] * y


def _merge_call(o, proj, x, mod, conv_w, w_o, w_co, w_out, seq):
    t, d = x.shape
    tm = TM_MERGE
    tiles_per_seq = seq // tm
    hb = tm // BF16_SUBLANES
    n_halo = t // BF16_SUBLANES
    cw_blocks = CONV_WIDTH
    const = dict(pipeline_mode=pl.Buffered(1))
    prev_map = lambda c: (lambda i: (jnp.maximum(i * hb - 1, 0), c))
    next_map = lambda c: (lambda i: (jnp.minimum((i + 1) * hb, n_halo - 1), c))
    return pl.pallas_call(
        functools.partial(_merge_kernel, tiles_per_seq=tiles_per_seq),
        grid=(t // tm,),
        in_specs=[
            pl.BlockSpec((tm, d), lambda i: (i, 0)),
            pl.BlockSpec((tm, d), lambda i: (i, COL_LA // d)),
            pl.BlockSpec((tm, d), lambda i: (i, COL_LC // d)),
            pl.BlockSpec((tm, cw_blocks), lambda i: (i, COL_GB // cw_blocks)),
            pl.BlockSpec((tm, cw_blocks), lambda i: (i, COL_GC // cw_blocks)),
            pl.BlockSpec((tm, cw_blocks), lambda i: (i, COL_U // cw_blocks)),
            pl.BlockSpec((BF16_SUBLANES, cw_blocks), prev_map(COL_GC // cw_blocks)),
            pl.BlockSpec((BF16_SUBLANES, cw_blocks), prev_map(COL_U // cw_blocks)),
            pl.BlockSpec((BF16_SUBLANES, cw_blocks), next_map(COL_GC // cw_blocks)),
            pl.BlockSpec((BF16_SUBLANES, cw_blocks), next_map(COL_U // cw_blocks)),
            pl.BlockSpec((tm, d), lambda i: (i, 0)),
            pl.BlockSpec((1, N_MOD, d), lambda i: (i // tiles_per_seq, 0, 0)),
            pl.BlockSpec(conv_w.shape, lambda i: (0, 0)),
            pl.BlockSpec(w_o.shape, lambda i: (0, 0), **const),
            pl.BlockSpec(w_co.shape, lambda i: (0, 0), **const),
            pl.BlockSpec(w_out.shape, lambda i: (0, 0), **const),
        ],
        out_specs=pl.BlockSpec((tm, d), lambda i: (i, 0)),
        out_shape=jax.ShapeDtypeStruct((t, d), F32),
        compiler_params=_cparams(("parallel",)),
        name="conv_merge_out",
    )(o, proj, proj, proj, proj, proj, proj, proj, proj, proj, x, mod, conv_w, w_o, w_co, w_out)


def _first_max(vals, iota, sentinel):
    m = jnp.max(vals, axis=0, keepdims=True)
    first = jnp.min(jnp.where(vals == m, iota, sentinel), axis=0, keepdims=True)
    return m, first, iota == first


def _route_kernel(x_ref, g_ref, mod_ref, wr_ref, br_ref, w1_ref, w3_ref, w2_ref,
                  h_ref, xp_ref, idx_ref, wts_ref):
    m = mod_ref[0]
    x = x_ref[...]
    h = (_rms(x) * g_ref[...]) * (1.0 + m[4:5]) + m[3:4]
    h_ref[...] = h
    hb = h.astype(BF16)
    a = jnp.dot(hb, w1_ref[...], preferred_element_type=F32)
    b = jnp.dot(hb, w3_ref[...], preferred_element_type=F32)
    shared = jnp.dot((_silu(a) * b).astype(BF16), w2_ref[...], preferred_element_type=F32)
    xp_ref[...] = x + m[5:6] * shared

    logits = lax.dot_general(wr_ref[...], h, (((1,), (1,)), ((), ())),
                             precision=lax.Precision.HIGHEST, preferred_element_type=F32)
    scores = jax.nn.sigmoid(logits)
    biased = scores + br_ref[...]
    tm = scores.shape[1]
    neg = -jnp.inf
    g = EXPERTS_PER_GROUP
    iota_g = lax.broadcasted_iota(jnp.int32, (g, tm), 0).astype(F32)
    group_rows = []
    for gi in range(N_GROUPS):
        blk = biased[gi * g:(gi + 1) * g, :]
        m1, _, hit = _first_max(blk, iota_g, g)
        m2 = jnp.max(jnp.where(hit, neg, blk), axis=0, keepdims=True)
        group_rows.append(m1 + m2)
    gs = jnp.concatenate(group_rows, axis=0)
    iota_n = lax.broadcasted_iota(jnp.int32, (N_GROUPS, tm), 0).astype(F32)
    chosen = jnp.zeros((N_GROUPS, tm), F32)
    for _ in range(TOPK_GROUPS):
        _, _, hit = _first_max(gs, iota_n, N_GROUPS)
        chosen = jnp.where(hit, 1.0, chosen)
        gs = jnp.where(hit, neg, gs)
    allowed = jnp.concatenate(
        [jnp.broadcast_to(chosen[gi:gi + 1, :], (g, tm)) for gi in range(N_GROUPS)], axis=0)
    cand = jnp.where(allowed > 0.0, biased, neg)
    iota_e = lax.broadcasted_iota(jnp.int32, (N_EXPERTS, tm), 0).astype(F32)
    idx_rows, w_rows = [], []
    for _ in range(TOP_K):
        _, first, hit = _first_max(cand, iota_e, N_EXPERTS)
        idx_rows.append(first)
        w_rows.append(jnp.sum(jnp.where(hit, scores, 0.0), axis=0, keepdims=True))
        cand = jnp.where(hit, neg, cand)
    w = jnp.concatenate(w_rows, axis=0)
    idx_ref[...] = jnp.concatenate(idx_rows, axis=0).astype(jnp.int32)
    wts_ref[...] = w / jnp.sum(w, axis=0, keepdims=True) * ROUTED_SCALE


def _route_call(x, g, mod, wr_t, br_col, w1s, w3s, w2s, seq):
    t, d = x.shape
    tm = TM_ROUTE
    tiles_per_seq = seq // tm
    return pl.pallas_call(
        _route_kernel,
        grid=(t // tm,),
        in_specs=[
            pl.BlockSpec((tm, d), lambda i: (i, 0)),
            pl.BlockSpec((1, d), lambda i: (0, 0)),
            pl.BlockSpec((1, N_MOD, d), lambda i: (i // tiles_per_seq, 0, 0)),
            pl.BlockSpec(wr_t.shape, lambda i: (0, 0)),
            pl.BlockSpec(br_col.shape, lambda i: (0, 0)),
            pl.BlockSpec(w1s.shape, lambda i: (0, 0)),
            pl.BlockSpec(w3s.shape, lambda i: (0, 0)),
            pl.BlockSpec(w2s.shape, lambda i: (0, 0)),
        ],
        out_specs=[
            pl.BlockSpec((tm, d), lambda i: (i, 0)),
            pl.BlockSpec((tm, d), lambda i: (i, 0)),
            pl.BlockSpec((TOP_K, tm), lambda i: (0, i)),
            pl.BlockSpec((TOP_K, tm), lambda i: (0, i)),
        ],
        out_shape=[
            jax.ShapeDtypeStruct((t, d), F32),
            jax.ShapeDtypeStruct((t, d), F32),
            jax.ShapeDtypeStruct((TOP_K, t), jnp.int32),
            jax.ShapeDtypeStruct((TOP_K, t), F32),
        ],
        compiler_params=_cparams(("parallel",)),
        name="moe_route",
    )(x, g, mod, wr_t, br_col, w1s, w3s, w2s)


def _expert_kernel(te_ref, nact_ref, tok_next_ref, tok_first_ref, dst_ref, h_hbm, rw_ref,
                   w1_ref, w3_ref, w2_ref, y_hbm,
                   xbuf, ybuf, w1b, w3b, w2b, gsem, ssem):
    i = pl.program_id(0)
    nact = nact_ref[0]
    slot = i % 2
    tm = xbuf.shape[1]

    def gather_copy(tok, r, s):
        return pltpu.make_async_copy(h_hbm.at[pl.ds(tok, 1)], xbuf.at[s, pl.ds(r, 1)], gsem.at[s])

    def scatter_copy(dst, r, s):
        return pltpu.make_async_copy(ybuf.at[s, pl.ds(r, 1)], y_hbm.at[pl.ds(dst, 1)], ssem.at[s])

    def start_gather(tok_ref, s):
        def body(r, carry):
            gather_copy(tok_ref[0, 0, r], r, s).start()
            return carry
        lax.fori_loop(0, tm, body, 0, unroll=8)

    def wait_rows(copy_fn, s):
        def body(r, carry):
            copy_fn(0, r, s).wait()
            return carry
        lax.fori_loop(0, tm, body, 0, unroll=8)

    @pl.when(i == 0)
    def _():
        start_gather(tok_first_ref, 0)

    @pl.when(i + 1 < nact)
    def _():
        start_gather(tok_next_ref, 1 - slot)

    @pl.when(i < nact)
    def _():
        @pl.when(i >= 2)
        def _():
            wait_rows(scatter_copy, slot)

        prev = jnp.maximum(i - 1, 0)
        @pl.when((i == 0) | (te_ref[i] != te_ref[prev]))
        def _():
            w1b[...] = w1_ref[0, 0].astype(BF16)
            w3b[...] = w3_ref[0, 0].astype(BF16)
            w2b[...] = w2_ref[0, 0].astype(BF16)

        wait_rows(gather_copy, slot)
        x = xbuf[slot].astype(BF16)
        a = jnp.dot(x, w1b[...], preferred_element_type=F32)
        b = jnp.dot(x, w3b[...], preferred_element_type=F32)
        y = jnp.dot((_silu(a) * b).astype(BF16), w2b[...], preferred_element_type=F32)
        ybuf[slot] = y * rw_ref[...]

        def body(r, carry):
            scatter_copy(dst_ref[0, 0, r], r, slot).start()
            return carry
        lax.fori_loop(0, tm, body, 0, unroll=8)

    @pl.when(i == nact - 1)
    def _():
        wait_rows(scatter_copy, slot)

        @pl.when(i >= 1)
        def _():
            wait_rows(scatter_copy, 1 - slot)


def _expert_call(layer, tile_e, nact, row_tok3, row_dst3, h, row_w, w1_e, w3_e, w2_e, n_slots_rows):
    tm = TM_EXP
    n_tiles = tile_e.shape[0]
    d = h.shape[1]
    de = w1_e.shape[-1]
    grid_spec = pltpu.PrefetchScalarGridSpec(
        num_scalar_prefetch=2,
        grid=(n_tiles,),
        in_specs=[
            pl.BlockSpec((1, 1, tm), lambda i, te, na: (jnp.minimum(i + 1, n_tiles - 1), 0, 0),
                         memory_space=pltpu.SMEM),
            pl.BlockSpec((1, 1, tm), lambda i, te, na: (0, 0, 0), memory_space=pltpu.SMEM),
            pl.BlockSpec((1, 1, tm), lambda i, te, na: (i, 0, 0), memory_space=pltpu.SMEM),
            pl.BlockSpec(memory_space=pl.ANY),
            pl.BlockSpec((tm, 1), lambda i, te, na: (i, 0)),
            pl.BlockSpec((1, 1, d, de), lambda i, te, na: (layer, te[i], 0, 0)),
            pl.BlockSpec((1, 1, d, de), lambda i, te, na: (layer, te[i], 0, 0)),
            pl.BlockSpec((1, 1, de, d), lambda i, te, na: (layer, te[i], 0, 0)),
        ],
        out_specs=pl.BlockSpec(memory_space=pl.ANY),
        scratch_shapes=[
            pltpu.VMEM((2, tm, d), F32),
            pltpu.VMEM((2, tm, d), F32),
            pltpu.VMEM((d, de), BF16),
            pltpu.VMEM((d, de), BF16),
            pltpu.VMEM((de, d), BF16),
            pltpu.SemaphoreType.DMA((2,)),
            pltpu.SemaphoreType.DMA((2,)),
        ],
    )
    return pl.pallas_call(
        _expert_kernel,
        grid_spec=grid_spec,
        out_shape=jax.ShapeDtypeStruct((n_slots_rows, d), F32),
        compiler_params=_cparams(("arbitrary",)),
        name="moe_experts",
    )(tile_e, nact, row_tok3, row_tok3, row_dst3, h, row_w, w1_e, w3_e, w2_e)


def _combine_kernel(*refs, final):
    y_refs = refs[:TOP_K]
    xp_ref, mod_ref, gf_ref, out_ref = refs[TOP_K:]
    acc = y_refs[0][...]
    for r in y_refs[1:]:
        acc = acc + r[...]
    x = xp_ref[...] + mod_ref[0][5:6] * acc
    if final:
        x = _rms(x) * gf_ref[...]
    out_ref[...] = x


def _combine_call(y_slots, xp, mod, g_final, seq, final):
    t, d = xp.shape
    tm = TM_COMB
    tiles_per_seq = seq // tm
    nt = t // tm
    y_specs = [pl.BlockSpec((tm, d), (lambda i, k=k: (k * nt + i, 0))) for k in range(TOP_K)]
    return pl.pallas_call(
        functools.partial(_combine_kernel, final=final),
        grid=(nt,),
        in_specs=y_specs + [
            pl.BlockSpec((tm, d), lambda i: (i, 0)),
            pl.BlockSpec((1, N_MOD, d), lambda i: (i // tiles_per_seq, 0, 0)),
            pl.BlockSpec((1, d), lambda i: (0, 0)),
        ],
        out_specs=pl.BlockSpec((tm, d), lambda i: (i, 0)),
        out_shape=jax.ShapeDtypeStruct((t, d), F32),
        compiler_params=_cparams(("parallel",)),
        name="moe_combine",
    )(*([y_slots] * TOP_K), xp, mod, g_final)


def _dispatch_plan(idx_t, wts_t, n_tok):
    tm = TM_EXP
    n_assign = TOP_K * n_tok
    n_tiles = n_assign // tm + N_EXPERTS
    n_rows = n_tiles * tm
    e_flat = idx_t.reshape(-1)
    w_flat = wts_t.reshape(-1)
    order = jnp.argsort(e_flat, stable=True).astype(jnp.int32)
    counts = jnp.sum((e_flat[None, :] == jnp.arange(N_EXPERTS, dtype=jnp.int32)[:, None]).astype(jnp.int32), axis=1)
    start = jnp.cumsum(counts) - counts
    padded = (counts + tm - 1) // tm * tm
    pend = jnp.cumsum(padded)
    pstart = pend - padded
    tile_start = jnp.arange(n_tiles, dtype=jnp.int32) * tm
    tile_e = jnp.minimum(jnp.searchsorted(pend, tile_start, side='right'), N_EXPERTS - 1).astype(jnp.int32)
    nact = (pend[-1] // tm).astype(jnp.int32).reshape(1)
    rows = jnp.arange(n_rows, dtype=jnp.int32)
    row_e = jnp.repeat(tile_e, tm)
    off = rows - pstart[row_e]
    valid = (off < counts[row_e]) & (rows < pend[-1])
    src = order[jnp.clip(start[row_e] + off, 0, n_assign - 1)]
    row_tok = jnp.where(valid, src % n_tok, 0).astype(jnp.int32)
    row_dst = jnp.where(valid, src, n_assign + rows % (2 * tm)).astype(jnp.int32)
    row_w = jnp.where(valid, w_flat[src], 0.0).astype(F32)
    return (tile_e, nact, row_tok.reshape(n_tiles, 1, tm), row_dst.reshape(n_tiles, 1, tm),
            row_w.reshape(n_rows, 1), n_assign + 2 * tm)


def _reorder_w_in(w):
    cq, ckv, kpe, gb, gc, u, la, lc = jnp.split(
        w, [512, 768, 832, 1856, 2880, 3904, 5952], axis=1)
    pad = jnp.zeros((w.shape[0], N_PROJ - 8000), w.dtype)
    return jnp.concatenate([la, lc, gb, gc, u, cq, ckv, kpe, pad], axis=1).astype(BF16)


def kernel(x, c, positions, w_ada, b_ada, g_mix, w_in, g_q, w_uq, g_kv, w_ukv, w_o_attn, conv_w,
           w_conv_out, w_out, g_ffn, w_router, b_router, w1_e, w3_e, w2_e, w1_s, w3_s, w2_s, g_final):
    batch, seq, d = x.shape
    depth = w_ada.shape[0]
    t = batch * seq
    xf = x.reshape(t, d)

    c8 = jnp.zeros((8, d), F32).at[:batch].set(c)
    mod_all = _ada_call(c8, w_ada, b_ada.reshape(depth, 1, N_MOD * d))

    inv = 1.0 / (ROPE_THETA ** (jnp.arange(0, QK_ROPE, 2, dtype=F32) / QK_ROPE))
    inv128 = jnp.concatenate([inv, inv, jnp.zeros((LANES - QK_ROPE,), F32)]).reshape(1, LANES)
    cos_t, sin_t = _rope_call(positions.reshape(t, 1), inv128)

    for l in range(depth):
        mod = mod_all[l, :batch].reshape(batch, N_MOD, d)
        wq = jnp.pad(w_uq[l], ((0, 0), (0, 0), (0, HEAD_PAD - QK_HEAD))).reshape(
            Q_LORA, N_HEADS * HEAD_PAD).astype(BF16)
        wk = w_ukv[l][:, :, :QK_NOPE].reshape(KV_LORA, N_HEADS * QK_NOPE).astype(BF16)
        wv = w_ukv[l][:, :, QK_NOPE:].reshape(KV_LORA, N_HEADS * V_HEAD).astype(BF16)

        proj = _in_call(xf, g_mix[l].reshape(1, d), mod, _reorder_w_in(w_in[l]), seq)
        q, k, v = _qkv_call(proj, cos_t, sin_t, g_q[l].reshape(1, Q_LORA),
                            g_kv[l].reshape(1, KV_LORA), wq, wk, wv)
        o = _attn_call(q, k, v, batch, seq)
        xf = _merge_call(o, proj, xf, mod, conv_w[l].reshape(3, CONV_WIDTH),
                         w_o_attn[l].astype(BF16), w_conv_out[l].astype(BF16),
                         w_out[l].astype(BF16), seq)

        h, xp, idx_t, wts_t = _route_call(
            xf, g_ffn[l].reshape(1, d), mod, w_router[l].T, b_router[l].reshape(N_EXPERTS, 1),
            w1_s[l].astype(BF16), w3_s[l].astype(BF16), w2_s[l].astype(BF16), seq)
        tile_e, nact, row_tok3, row_dst3, row_w, n_slot_rows = _dispatch_plan(idx_t, wts_t, t)
        y_slots = _expert_call(l, tile_e, nact, row_tok3, row_dst3, h, row_w,
                               w1_e, w3_e, w2_e, n_slot_rows)
        xf = _combine_call(y_slots, xp, mod, g_final.reshape(1, d), seq, final=(l == depth - 1))

    return xf.reshape(batch, seq, d)
```

```python
import functools

import jax
import jax.numpy as jnp
from jax import lax
from jax.experimental import pallas as pl
from jax.experimental.pallas import tpu as pltpu

F32 = jnp.float32
BF16 = jnp.bfloat16

N_HEADS = 16
Q_LORA = 512
KV_LORA = 256
QK_NOPE = 128
QK_ROPE = 64
V_HEAD = 128
QK_HEAD = QK_NOPE + QK_ROPE
ROPE_THETA = 10000.0
CONV_WIDTH = 1024
N_EXPERTS = 64
TOP_K = 8
N_GROUPS = 8
TOPK_GROUPS = 4
EXPERTS_PER_GROUP = N_EXPERTS // N_GROUPS
ROUTED_SCALE = 2.5
N_MOD = 6
EPS = 1e-6

LANES = 128
BF16_SUBLANES = 16
HEAD_PAD = 2 * LANES

COL_LA = 0
COL_LC = 2048
COL_GB = 4096
COL_GC = 5120
COL_U = 6144
COL_CQ = 7168
COL_CKV = 7680
COL_KPE = 7936
N_PROJ = 8064

VMEM_LIMIT = 56 * 1024 * 1024

TM_IN = 512
TN_IN = 1152
TM_QKV = 512
TQ_ATTN = 512
TM_MERGE = 256
TM_ROUTE = 512
TM_EXP = 256
TM_TOK = 128
TN_ADA = 1024
DMA_UNROLL = 8


def _cparams(sem):
    return pltpu.CompilerParams(dimension_semantics=sem, vmem_limit_bytes=VMEM_LIMIT)


def _rms(x):
    return x * lax.rsqrt(jnp.mean(x * x, axis=-1, keepdims=True) + EPS)


def _silu(x):
    return x * jax.nn.sigmoid(x)


def _ada_kernel(c_ref, w_ref, b_ref, o_ref):
    ca = _silu(c_ref[...]).astype(BF16)
    o_ref[0] = jnp.dot(ca, w_ref[0].astype(BF16), preferred_element_type=F32) + b_ref[0]


def _ada_call(c8, w_ada, b_ada3):
    depth, d, n = w_ada.shape
    return pl.pallas_call(
        _ada_kernel,
        grid=(depth, n // TN_ADA),
        in_specs=[
            pl.BlockSpec((8, d), lambda l, j: (0, 0)),
            pl.BlockSpec((1, d, TN_ADA), lambda l, j: (l, 0, j)),
            pl.BlockSpec((1, 1, TN_ADA), lambda l, j: (l, 0, j)),
        ],
        out_specs=pl.BlockSpec((1, 8, TN_ADA), lambda l, j: (l, 0, j)),
        out_shape=jax.ShapeDtypeStruct((depth, 8, n), F32),
        compiler_params=_cparams(("parallel", "parallel")),
        name="ada_mod",
    )(c8, w_ada, b_ada3)


def _rope_kernel(pos_ref, inv_ref, cos_ref, sin_ref):
    ang = pos_ref[...].astype(F32) * inv_ref[...]
    lane = lax.broadcasted_iota(jnp.int32, ang.shape, 1)
    half = QK_ROPE // 2
    cos_ref[...] = jnp.where(lane < QK_ROPE, jnp.cos(ang), 0.0)
    s = jnp.sin(ang)
    sin_ref[...] = jnp.where(lane < half, -s, jnp.where(lane < QK_ROPE, s, 0.0))


def _rope_call(pos_col, inv128):
    t = pos_col.shape[0]
    tm = 1024
    return pl.pallas_call(
        _rope_kernel,
        grid=(t // tm,),
        in_specs=[pl.BlockSpec((tm, 1), lambda i: (i, 0)),
                  pl.BlockSpec((1, LANES), lambda i: (0, 0))],
        out_specs=[pl.BlockSpec((tm, LANES), lambda i: (i, 0)),
                   pl.BlockSpec((tm, LANES), lambda i: (i, 0))],
        out_shape=[jax.ShapeDtypeStruct((t, LANES), F32)] * 2,
        compiler_params=_cparams(("parallel",)),
        name="rope_tables",
    )(pos_col, inv128)


def _in_kernel(x_ref, g_ref, mod_ref, w_ref, o_ref, h_ref):
    @pl.when(pl.program_id(1) == 0)
    def _():
        m = mod_ref[0]
        h = _rms(x_ref[...]) * g_ref[...]
        h_ref[...] = (h * (1.0 + m[1:2]) + m[0:1]).astype(BF16)

    o_ref[...] = jnp.dot(h_ref[...], w_ref[...], preferred_element_type=F32).astype(o_ref.dtype)


def _in_call(x, g, mod, w_in_r, seq):
    t, d = x.shape
    tiles_per_seq = seq // TM_IN
    return pl.pallas_call(
        _in_kernel,
        grid=(t // TM_IN, N_PROJ // TN_IN),
        in_specs=[
            pl.BlockSpec((TM_IN, d), lambda i, j: (i, 0)),
            pl.BlockSpec((1, d), lambda i, j: (0, 0)),
            pl.BlockSpec((1, N_MOD, d), lambda i, j: (i // tiles_per_seq, 0, 0)),
            pl.BlockSpec((d, TN_IN), lambda i, j: (0, j)),
        ],
        out_specs=pl.BlockSpec((TM_IN, TN_IN), lambda i, j: (i, j)),
        out_shape=jax.ShapeDtypeStruct((t, N_PROJ), BF16),
        scratch_shapes=[pltpu.VMEM((TM_IN, d), BF16)],
        compiler_params=_cparams(("parallel", "arbitrary")),
        name="in_proj",
    )(x, g, mod, w_in_r)


def _rope128(t, cos, sin_signed):
    lane = lax.broadcasted_iota(jnp.int32, t.shape, 1)
    half = QK_ROPE // 2
    partner = jnp.where(lane < half, pltpu.roll(t, LANES - half, 1), pltpu.roll(t, half, 1))
    return t * cos + partner * sin_signed


def _qkv_kernel(cq_ref, ckv_ref, kpe_ref, cos_ref, sin_ref, gq_ref, gkv_ref,
                wq_ref, wk_ref, wv_ref, q_ref, k_ref, v_ref):
    cos = cos_ref[...]
    sin = sin_ref[...]
    qn = (_rms(cq_ref[...].astype(F32)) * gq_ref[...]).astype(BF16)
    q = jnp.dot(qn, wq_ref[...], preferred_element_type=F32) * (QK_HEAD ** -0.5)
    cn = (_rms(ckv_ref[...].astype(F32)) * gkv_ref[...]).astype(BF16)
    kn = jnp.dot(cn, wk_ref[...], preferred_element_type=F32)
    v_ref[...] = jnp.dot(cn, wv_ref[...], preferred_element_type=F32).astype(BF16)
    kr = _rope128(kpe_ref[...].astype(F32), cos, sin).astype(BF16)
    for h in range(N_HEADS):
        base = h * HEAD_PAD
        q_ref[:, base:base + LANES] = q[:, base:base + LANES].astype(BF16)
        q_ref[:, base + LANES:base + HEAD_PAD] = _rope128(
            q[:, base + LANES:base + HEAD_PAD], cos, sin).astype(BF16)
        k_ref[:, base:base + LANES] = kn[:, h * QK_NOPE:(h + 1) * QK_NOPE].astype(BF16)
        k_ref[:, base + LANES:base + HEAD_PAD] = kr


def _qkv_call(proj, cos, sin, gq, gkv, wq, wk, wv):
    t = proj.shape[0]
    tm = TM_QKV
    return pl.pallas_call(
        _qkv_kernel,
        grid=(t // tm,),
        in_specs=[
            pl.BlockSpec((tm, Q_LORA), lambda i: (i, COL_CQ // Q_LORA)),
            pl.BlockSpec((tm, KV_LORA), lambda i: (i, COL_CKV // KV_LORA)),
            pl.BlockSpec((tm, LANES), lambda i: (i, COL_KPE // LANES)),
            pl.BlockSpec((tm, LANES), lambda i: (i, 0)),
            pl.BlockSpec((tm, LANES), lambda i: (i, 0)),
            pl.BlockSpec((1, Q_LORA), lambda i: (0, 0)),
            pl.BlockSpec((1, KV_LORA), lambda i: (0, 0)),
            pl.BlockSpec(wq.shape, lambda i: (0, 0)),
            pl.BlockSpec(wk.shape, lambda i: (0, 0)),
            pl.BlockSpec(wv.shape, lambda i: (0, 0)),
        ],
        out_specs=[
            pl.BlockSpec((tm, N_HEADS * HEAD_PAD), lambda i: (i, 0)),
            pl.BlockSpec((tm, N_HEADS * HEAD_PAD), lambda i: (i, 0)),
            pl.BlockSpec((tm, N_HEADS * V_HEAD), lambda i: (i, 0)),
        ],
        out_shape=[
            jax.ShapeDtypeStruct((t, N_HEADS * HEAD_PAD), BF16),
            jax.ShapeDtypeStruct((t, N_HEADS * HEAD_PAD), BF16),
            jax.ShapeDtypeStruct((t, N_HEADS * V_HEAD), BF16),
        ],
        compiler_params=_cparams(("parallel",)),
        name="qkv_build",
    )(proj, proj, proj, cos, sin, gq, gkv, wq, wk, wv)


def _attn_kernel(q_ref, k_ref, v_ref, o_ref):
    s = lax.dot_general(q_ref[...], k_ref[...], (((1,), (1,)), ((), ())),
                        preferred_element_type=F32)
    m = jnp.max(s, axis=-1, keepdims=True)
    p = jnp.exp(s - m)
    l = jnp.sum(p, axis=-1, keepdims=True)
    o = jnp.dot(p.astype(BF16), v_ref[...], preferred_element_type=F32)
    o_ref[...] = (o / l).astype(o_ref.dtype)


def _attn_call(q, k, v, batch, seq):
    t = q.shape[0]
    nq = seq // TQ_ATTN
    return pl.pallas_call(
        _attn_kernel,
        grid=(batch, N_HEADS, nq),
        in_specs=[
            pl.BlockSpec((TQ_ATTN, HEAD_PAD), lambda b, h, i: (b * nq + i, h)),
            pl.BlockSpec((seq, HEAD_PAD), lambda b, h, i: (b, h)),
            pl.BlockSpec((seq, V_HEAD), lambda b, h, i: (b, h)),
        ],
        out_specs=pl.BlockSpec((TQ_ATTN, V_HEAD), lambda b, h, i: (b * nq + i, h)),
        out_shape=jax.ShapeDtypeStruct((t, N_HEADS * V_HEAD), BF16),
        compiler_params=_cparams(("parallel", "parallel", "arbitrary")),
        name="mla_attention",
    )(q, k, v)


def _merge_kernel(o_ref, la_ref, lc_ref, gb_ref, gc_ref, u_ref, gcp_ref, up_ref, gcn_ref, un_ref,
                  x_ref, mod_ref, cw_ref, wo_ref, wco_ref, wout_ref, out_ref, *, tiles_per_seq):
    i = pl.program_id(0)
    tm = x_ref.shape[0]
    pos = i % tiles_per_seq
    has_prev = (pos != 0).astype(F32)
    has_next = (pos != tiles_per_seq - 1).astype(F32)
    z = gc_ref[...].astype(F32) * u_ref[...].astype(F32)
    last = BF16_SUBLANES - 1
    z_before = gcp_ref[last:last + 1, :].astype(F32) * up_ref[last:last + 1, :].astype(F32) * has_prev
    z_after = gcn_ref[0:1, :].astype(F32) * un_ref[0:1, :].astype(F32) * has_next
    row = lax.broadcasted_iota(jnp.int32, z.shape, 0)
    z_prev = jnp.where(row == 0, z_before, pltpu.roll(z, 1, 0))
    z_next = jnp.where(row == tm - 1, z_after, pltpu.roll(z, tm - 1, 0))
    cw = cw_ref[...]
    conv = cw[0:1] * z_prev + cw[1:2] * z + cw[2:3] * z_next
    y_conv = jnp.dot((gb_ref[...].astype(F32) * conv).astype(BF16), wco_ref[...],
                     preferred_element_type=F32)
    y_attn = jnp.dot(o_ref[...], wo_ref[...], preferred_element_type=F32)
    mix = (jax.nn.sigmoid(la_ref[...].astype(F32)) * y_attn
           + jax.nn.sigmoid(lc_ref[...].astype(F32)) * y_conv)
    y = jnp.dot(mix.astype(BF16), wout_ref[...], preferred_element_type=F32)
    out_ref[...] = x_ref[...] + mod_ref[0][2:3] * y


def _merge_call(o, proj, x, mod, conv_w, w_o, w_co, w_out, seq):
    t, d = x.shape
    tm = TM_MERGE
    tiles_per_seq = seq // tm
    hb = tm // BF16_SUBLANES
    n_halo = t // BF16_SUBLANES
    cw_blocks = CONV_WIDTH
    const = dict(pipeline_mode=pl.Buffered(1))
    prev_map = lambda c: (lambda i: (jnp.maximum(i * hb - 1, 0), c))
    next_map = lambda c: (lambda i: (jnp.minimum((i + 1) * hb, n_halo - 1), c))
    return pl.pallas_call(
        functools.partial(_merge_kernel, tiles_per_seq=tiles_per_seq),
        grid=(t // tm,),
        in_specs=[
            pl.BlockSpec((tm, d), lambda i: (i, 0)),
            pl.BlockSpec((tm, d), lambda i: (i, COL_LA // d)),
            pl.BlockSpec((tm, d), lambda i: (i, COL_LC // d)),
            pl.BlockSpec((tm, cw_blocks), lambda i: (i, COL_GB // cw_blocks)),
            pl.BlockSpec((tm, cw_blocks), lambda i: (i, COL_GC // cw_blocks)),
            pl.BlockSpec((tm, cw_blocks), lambda i: (i, COL_U // cw_blocks)),
            pl.BlockSpec((BF16_SUBLANES, cw_blocks), prev_map(COL_GC // cw_blocks)),
            pl.BlockSpec((BF16_SUBLANES, cw_blocks), prev_map(COL_U // cw_blocks)),
            pl.BlockSpec((BF16_SUBLANES, cw_blocks), next_map(COL_GC // cw_blocks)),
            pl.BlockSpec((BF16_SUBLANES, cw_blocks), next_map(COL_U // cw_blocks)),
            pl.BlockSpec((tm, d), lambda i: (i, 0)),
            pl.BlockSpec((1, N_MOD, d), lambda i: (i // tiles_per_seq, 0, 0)),
            pl.BlockSpec(conv_w.shape, lambda i: (0, 0)),
            pl.BlockSpec(w_o.shape, lambda i: (0, 0), **const),
            pl.BlockSpec(w_co.shape, lambda i: (0, 0), **const),
            pl.BlockSpec(w_out.shape, lambda i: (0, 0), **const),
        ],
        out_specs=pl.BlockSpec((tm, d), lambda i: (i, 0)),
        out_shape=jax.ShapeDtypeStruct((t, d), F32),
        compiler_params=_cparams(("parallel",)),
        name="conv_merge_out",
    )(o, proj, proj, proj, proj, proj, proj, proj, proj, proj, x, mod, conv_w, w_o, w_co, w_out)


def _first_max(vals, iota, sentinel):
    m = jnp.max(vals, axis=0, keepdims=True)
    first = jnp.min(jnp.where(vals == m, iota, sentinel), axis=0, keepdims=True)
    return m, first, iota == first


def _route_kernel(x_ref, g_ref, mod_ref, wr_ref, br_ref, w1_ref, w3_ref, w2_ref, tri_ref,
                  h_ref, xp_ref, idx_ref, wts_ref, rank_ref, cnt_ref, run_ref):
    @pl.when(pl.program_id(0) == 0)
    def _():
        run_ref[...] = jnp.zeros_like(run_ref)

    m = mod_ref[0]
    x = x_ref[...]
    h = (_rms(x) * g_ref[...]) * (1.0 + m[4:5]) + m[3:4]
    h_ref[...] = h
    hb = h.astype(BF16)
    a = jnp.dot(hb, w1_ref[...], preferred_element_type=F32)
    b = jnp.dot(hb, w3_ref[...], preferred_element_type=F32)
    shared = jnp.dot((_silu(a) * b).astype(BF16), w2_ref[...], preferred_element_type=F32)
    xp_ref[...] = x + m[5:6] * shared

    logits = lax.dot_general(wr_ref[...], h, (((1,), (1,)), ((), ())),
                             precision=lax.Precision.HIGHEST, preferred_element_type=F32)
    scores = jax.nn.sigmoid(logits)
    biased = scores + br_ref[...]
    tm = scores.shape[1]
    neg = -jnp.inf
    g = EXPERTS_PER_GROUP
    iota_g = lax.broadcasted_iota(jnp.int32, (g, tm), 0).astype(F32)
    group_rows = []
    for gi in range(N_GROUPS):
        blk = biased[gi * g:(gi + 1) * g, :]
        m1, _, hit = _first_max(blk, iota_g, g)
        m2 = jnp.max(jnp.where(hit, neg, blk), axis=0, keepdims=True)
        group_rows.append(m1 + m2)
    gs = jnp.concatenate(group_rows, axis=0)
    iota_n = lax.broadcasted_iota(jnp.int32, (N_GROUPS, tm), 0).astype(F32)
    chosen = jnp.zeros((N_GROUPS, tm), F32)
    for _ in range(TOPK_GROUPS):
        _, _, hit = _first_max(gs, iota_n, N_GROUPS)
        chosen = jnp.where(hit, 1.0, chosen)
        gs = jnp.where(hit, neg, gs)
    allowed = jnp.concatenate(
        [jnp.broadcast_to(chosen[gi:gi + 1, :], (g, tm)) for gi in range(N_GROUPS)], axis=0)
    cand = jnp.where(allowed > 0.0, biased, neg)
    iota_e = lax.broadcasted_iota(jnp.int32, (N_EXPERTS, tm), 0).astype(F32)

    run = run_ref[:, 0:1]
    tri = tri_ref[...]
    idx_rows, w_rows, rank_rows = [], [], []
    for _ in range(TOP_K):
        _, first, hit = _first_max(cand, iota_e, N_EXPERTS)
        idx_rows.append(first)
        w_rows.append(jnp.sum(jnp.where(hit, scores, 0.0), axis=0, keepdims=True))
        cand = jnp.where(hit, neg, cand)
        prefix = jnp.dot(jnp.where(hit, 1.0, 0.0).astype(BF16), tri, preferred_element_type=F32)
        rank_rows.append(jnp.sum(jnp.where(hit, run + prefix - 1.0, 0.0), axis=0, keepdims=True))
        run = run + prefix[:, tm - 1:tm]
    w = jnp.concatenate(w_rows, axis=0)
    idx_ref[...] = jnp.concatenate(idx_rows, axis=0).astype(jnp.int32)
    wts_ref[...] = w / jnp.sum(w, axis=0, keepdims=True) * ROUTED_SCALE
    rank_ref[...] = jnp.concatenate(rank_rows, axis=0).astype(jnp.int32)
    run_full = jnp.broadcast_to(run, run_ref.shape)
    run_ref[...] = run_full
    cnt_ref[...] = run_full


def _route_call(x, g, mod, wr_t, br_col, w1s, w3s, w2s, seq):
    t, d = x.shape
    tm = TM_ROUTE
    tiles_per_seq = seq // tm
    tri = (jnp.arange(tm)[:, None] <= jnp.arange(tm)[None, :]).astype(BF16)
    return pl.pallas_call(
        _route_kernel,
        grid=(t // tm,),
        in_specs=[
            pl.BlockSpec((tm, d), lambda i: (i, 0)),
            pl.BlockSpec((1, d), lambda i: (0, 0)),
            pl.BlockSpec((1, N_MOD, d), lambda i: (i // tiles_per_seq, 0, 0)),
            pl.BlockSpec(wr_t.shape, lambda i: (0, 0)),
            pl.BlockSpec(br_col.shape, lambda i: (0, 0)),
            pl.BlockSpec(w1s.shape, lambda i: (0, 0)),
            pl.BlockSpec(w3s.shape, lambda i: (0, 0)),
            pl.BlockSpec(w2s.shape, lambda i: (0, 0)),
            pl.BlockSpec((tm, tm), lambda i: (0, 0)),
        ],
        out_specs=[
            pl.BlockSpec((tm, d), lambda i: (i, 0)),
            pl.BlockSpec((tm, d), lambda i: (i, 0)),
            pl.BlockSpec((TOP_K, tm), lambda i: (0, i)),
            pl.BlockSpec((TOP_K, tm), lambda i: (0, i)),
            pl.BlockSpec((TOP_K, tm), lambda i: (0, i)),
            pl.BlockSpec((N_EXPERTS, LANES), lambda i: (0, 0)),
        ],
        out_shape=[
            jax.ShapeDtypeStruct((t, d), F32),
            jax.ShapeDtypeStruct((t, d), F32),
            jax.ShapeDtypeStruct((TOP_K, t), jnp.int32),
            jax.ShapeDtypeStruct((TOP_K, t), F32),
            jax.ShapeDtypeStruct((TOP_K, t), jnp.int32),
            jax.ShapeDtypeStruct((N_EXPERTS, LANES), F32),
        ],
        scratch_shapes=[pltpu.VMEM((N_EXPERTS, LANES), F32)],
        compiler_params=_cparams(("arbitrary",)),
        name="moe_route",
    )(x, g, mod, wr_t, br_col, w1s, w3s, w2s, tri)


def _row_plan(idx_t, rank_t, cnt, n_tok):
    tm = TM_EXP
    n_tiles = TOP_K * n_tok // tm + N_EXPERTS
    counts = cnt[:, 0].astype(jnp.int32)
    padded = (counts + tm - 1) // tm * tm
    pend = jnp.cumsum(padded)
    pstart = pend - padded
    tile_start = jnp.arange(n_tiles, dtype=jnp.int32) * tm
    tile_e = jnp.minimum(jnp.sum((pend[None, :] <= tile_start[:, None]).astype(jnp.int32), axis=1),
                         N_EXPERTS - 1).astype(jnp.int32)
    nact = (pend[-1] // tm).astype(jnp.int32).reshape(1)
    zero_start = jnp.maximum(pend - tm, 0).astype(jnp.int32)
    experts = jnp.arange(N_EXPERTS, dtype=jnp.int32)[:, None, None]
    dest = rank_t + jnp.sum(jnp.where(idx_t[None] == experts, pstart[:, None, None], 0), axis=0)
    nt = n_tok // TM_TOK
    dest_tiles = dest.reshape(TOP_K, nt, TM_TOK).transpose(1, 0, 2).reshape(nt, 1, TOP_K * TM_TOK)
    return tile_e, nact, zero_start, dest_tiles.astype(jnp.int32), n_tiles * tm


def _dispatch_kernel(zs_ref, dest_ref, h_hbm, xs_hbm, zbuf, zsem, sem):
    i = pl.program_id(0)
    n = pl.num_programs(0)
    tm = TM_TOK
    tme = zbuf.shape[0]

    def row_copy(src_row, dst_row, s):
        return pltpu.make_async_copy(h_hbm.at[pl.ds(src_row, 1)], xs_hbm.at[pl.ds(dst_row, 1)],
                                     sem.at[s])

    def wait_rows(s):
        def body(j, carry):
            for _ in range(TOP_K):
                row_copy(0, 0, s).wait()
            return carry
        lax.fori_loop(0, tm, body, 0, unroll=DMA_UNROLL)

    @pl.when(i == 0)
    def _():
        zbuf[...] = jnp.zeros_like(zbuf)

        def zero_copy(e):
            start = pl.multiple_of(zs_ref[e], tme)
            return pltpu.make_async_copy(zbuf, xs_hbm.at[pl.ds(start, tme)], zsem)
        for e in range(N_EXPERTS):
            zero_copy(e).start()
        for e in range(N_EXPERTS):
            zero_copy(e).wait()

    @pl.when(i > 0)
    def _():
        wait_rows((i + 1) % 2)

    base = i * tm
    slot = i % 2

    def body(j, carry):
        for k in range(TOP_K):
            row_copy(base + j, dest_ref[0, 0, k * tm + j], slot).start()
        return carry
    lax.fori_loop(0, tm, body, 0, unroll=DMA_UNROLL)

    @pl.when(i == n - 1)
    def _():
        wait_rows(slot)


def _dispatch_call(zero_start, dest_tiles, h, n_rows):
    t, d = h.shape
    nt = t // TM_TOK
    grid_spec = pltpu.PrefetchScalarGridSpec(
        num_scalar_prefetch=1,
        grid=(nt,),
        in_specs=[
            pl.BlockSpec((1, 1, TOP_K * TM_TOK), lambda i, zs: (i, 0, 0), memory_space=pltpu.SMEM),
            pl.BlockSpec(memory_space=pl.ANY),
        ],
        out_specs=pl.BlockSpec(memory_space=pl.ANY),
        scratch_shapes=[
            pltpu.VMEM((TM_EXP, d), F32),
            pltpu.SemaphoreType.DMA(()),
            pltpu.SemaphoreType.DMA((2,)),
        ],
    )
    return pl.pallas_call(
        _dispatch_kernel,
        grid_spec=grid_spec,
        out_shape=jax.ShapeDtypeStruct((n_rows, d), F32),
        compiler_params=_cparams(("arbitrary",)),
        name="moe_dispatch",
    )(zero_start, dest_tiles, h)


def _expert_kernel(te_ref, nact_ref, x_ref, w1_ref, w3_ref, w2_ref, y_ref, w1b, w3b, w2b):
    i = pl.program_id(0)

    @pl.when(i < nact_ref[0])
    def _():
        prev = jnp.maximum(i - 1, 0)

        @pl.when((i == 0) | (te_ref[i] != te_ref[prev]))
        def _():
            w1b[...] = w1_ref[0, 0].astype(BF16)
            w3b[...] = w3_ref[0, 0].astype(BF16)
            w2b[...] = w2_ref[0, 0].astype(BF16)

        x = x_ref[...].astype(BF16)
        a = jnp.dot(x, w1b[...], preferred_element_type=F32)
        b = jnp.dot(x, w3b[...], preferred_element_type=F32)
        y_ref[...] = jnp.dot((_silu(a) * b).astype(BF16), w2b[...], preferred_element_type=F32)


def _expert_call(layer, tile_e, nact, xs, w1_e, w3_e, w2_e):
    tm = TM_EXP
    n_rows, d = xs.shape
    n_tiles = n_rows // tm
    de = w1_e.shape[-1]
    row_map = lambda i, te, na: (jnp.minimum(i, na[0] - 1), 0)
    w_map = lambda i, te, na: (layer, te[jnp.minimum(i, na[0] - 1)], 0, 0)
    grid_spec = pltpu.PrefetchScalarGridSpec(
        num_scalar_prefetch=2,
        grid=(n_tiles,),
        in_specs=[
            pl.BlockSpec((tm, d), row_map),
            pl.BlockSpec((1, 1, d, de), w_map),
            pl.BlockSpec((1, 1, d, de), w_map),
            pl.BlockSpec((1, 1, de, d), w_map),
        ],
        out_specs=pl.BlockSpec((tm, d), row_map),
        scratch_shapes=[
            pltpu.VMEM((d, de), BF16),
            pltpu.VMEM((d, de), BF16),
            pltpu.VMEM((de, d), BF16),
        ],
    )
    return pl.pallas_call(
        _expert_kernel,
        grid_spec=grid_spec,
        out_shape=jax.ShapeDtypeStruct((n_rows, d), F32),
        compiler_params=_cparams(("arbitrary",)),
        name="moe_experts",
    )(tile_e, nact, xs, w1_e, w3_e, w2_e)


def _combine_kernel(dnext_ref, dfirst_ref, y_hbm, w_ref, xp_ref, mod_ref, gf_ref, out_ref,
                    ybuf, sem, *, final):
    i = pl.program_id(0)
    n = pl.num_programs(0)
    tm = TM_TOK
    slot = i % 2

    def row_copy(src_row, k, j, s):
        return pltpu.make_async_copy(y_hbm.at[pl.ds(src_row, 1)], ybuf.at[s, k, pl.ds(j, 1)],
                                     sem.at[s])

    def start_rows(d_ref, s):
        def body(j, carry):
            for k in range(TOP_K):
                row_copy(d_ref[0, 0, k * tm + j], k, j, s).start()
            return carry
        lax.fori_loop(0, tm, body, 0, unroll=DMA_UNROLL)

    def wait_rows(s):
        def body(j, carry):
            for k in range(TOP_K):
                row_copy(0, k, j, s).wait()
            return carry
        lax.fori_loop(0, tm, body, 0, unroll=DMA_UNROLL)

    @pl.when(i == 0)
    def _():
        start_rows(dfirst_ref, 0)

    for s in range(2):
        @pl.when((i + 1 < n) & (slot == 1 - s))
        def _():
            start_rows(dnext_ref, s)

    for s in range(2):
        @pl.when(slot == s)
        def _():
            wait_rows(s)
            w = w_ref[...]
            acc = ybuf[s, 0] * w[:, 0:1]
            for k in range(1, TOP_K):
                acc = acc + ybuf[s, k] * w[:, k:k + 1]
            x = xp_ref[...] + mod_ref[0][5:6] * acc
            if final:
                x = _rms(x) * gf_ref[...]
            out_ref[...] = x


def _combine_call(dest_tiles, ys, wts_col, xp, mod, g_final, seq, final):
    t, d = xp.shape
    tm = TM_TOK
    nt = t // tm
    tiles_per_seq = seq // tm
    return pl.pallas_call(
        functools.partial(_combine_kernel, final=final),
        grid=(nt,),
        in_specs=[
            pl.BlockSpec((1, 1, TOP_K * tm), lambda i: (jnp.minimum(i + 1, nt - 1), 0, 0),
                         memory_space=pltpu.SMEM),
            pl.BlockSpec((1, 1, TOP_K * tm), lambda i: (0, 0, 0), memory_space=pltpu.SMEM),
            pl.BlockSpec(memory_space=pl.ANY),
            pl.BlockSpec((tm, TOP_K), lambda i: (i, 0)),
            pl.BlockSpec((tm, d), lambda i: (i, 0)),
            pl.BlockSpec((1, N_MOD, d), lambda i: (i // tiles_per_seq, 0, 0)),
            pl.BlockSpec((1, d), lambda i: (0, 0)),
        ],
        out_specs=pl.BlockSpec((tm, d), lambda i: (i, 0)),
        out_shape=jax.ShapeDtypeStruct((t, d), F32),
        scratch_shapes=[
            pltpu.VMEM((2, TOP_K, tm, d), F32),
            pltpu.SemaphoreType.DMA((2,)),
        ],
        compiler_params=_cparams(("arbitrary",)),
        name="moe_combine",
    )(dest_tiles, dest_tiles, ys, wts_col, xp, mod, g_final)


def _reorder_w_in(w):
    cq, ckv, kpe, gb, gc, u, la, lc = jnp.split(
        w, [512, 768, 832, 1856, 2880, 3904, 5952], axis=1)
    pad = jnp.zeros((w.shape[0], N_PROJ - 8000), w.dtype)
    return jnp.concatenate([la, lc, gb, gc, u, cq, ckv, kpe, pad], axis=1).astype(BF16)


def kernel(x, c, positions, w_ada, b_ada, g_mix, w_in, g_q, w_uq, g_kv, w_ukv, w_o_attn, conv_w,
           w_conv_out, w_out, g_ffn, w_router, b_router, w1_e, w3_e, w2_e, w1_s, w3_s, w2_s, g_final):
    batch, seq, d = x.shape
    depth = w_ada.shape[0]
    t = batch * seq
    xf = x.reshape(t, d)

    c8 = jnp.zeros((8, d), F32).at[:batch].set(c)
    mod_all = _ada_call(c8, w_ada, b_ada.reshape(depth, 1, N_MOD * d))

    inv = 1.0 / (ROPE_THETA ** (jnp.arange(0, QK_ROPE, 2, dtype=F32) / QK_ROPE))
    inv128 = jnp.concatenate([inv, inv, jnp.zeros((LANES - QK_ROPE,), F32)]).reshape(1, LANES)
    cos_t, sin_t = _rope_call(positions.reshape(t, 1), inv128)

    for l in range(depth):
        mod = mod_all[l, :batch].reshape(batch, N_MOD, d)
        wq = jnp.pad(w_uq[l], ((0, 0), (0, 0), (0, HEAD_PAD - QK_HEAD))).reshape(
            Q_LORA, N_HEADS * HEAD_PAD).astype(BF16)
        wk = w_ukv[l][:, :, :QK_NOPE].reshape(KV_LORA, N_HEADS * QK_NOPE).astype(BF16)
        wv = w_ukv[l][:, :, QK_NOPE:].reshape(KV_LORA, N_HEADS * V_HEAD).astype(BF16)

        proj = _in_call(xf, g_mix[l].reshape(1, d), mod, _reorder_w_in(w_in[l]), seq)
        q, k, v = _qkv_call(proj, cos_t, sin_t, g_q[l].reshape(1, Q_LORA),
                            g_kv[l].reshape(1, KV_LORA), wq, wk, wv)
        o = _attn_call(q, k, v, batch, seq)
        xf = _merge_call(o, proj, xf, mod, conv_w[l].reshape(3, CONV_WIDTH),
                         w_o_attn[l].astype(BF16), w_conv_out[l].astype(BF16),
                         w_out[l].astype(BF16), seq)

        h, xp, idx_t, wts_t, rank_t, cnt = _route_call(
            xf, g_ffn[l].reshape(1, d), mod, w_router[l].T, b_router[l].reshape(N_EXPERTS, 1),
            w1_s[l].astype(BF16), w3_s[l].astype(BF16), w2_s[l].astype(BF16), seq)
        tile_e, nact, zero_start, dest_tiles, n_rows = _row_plan(idx_t, rank_t, cnt, t)
        xs = _dispatch_call(zero_start, dest_tiles, h, n_rows)
        ys = _expert_call(l, tile_e, nact, xs, w1_e, w3_e, w2_e)
        xf = _combine_call(dest_tiles, ys, wts_t.T, xp, mod, g_final.reshape(1, d), seq,
                           final=(l == depth - 1))

    return xf.reshape(batch, seq, d)
```

```python
import functools

import jax
import jax.numpy as jnp
from jax import lax
from jax.experimental import pallas as pl
from jax.experimental.pallas import tpu as pltpu

F32 = jnp.float32
BF16 = jnp.bfloat16

N_HEADS = 16
Q_LORA = 512
KV_LORA = 256
QK_NOPE = 128
QK_ROPE = 64
V_HEAD = 128
QK_HEAD = QK_NOPE + QK_ROPE
ROPE_THETA = 10000.0
CONV_WIDTH = 1024
N_EXPERTS = 64
TOP_K = 8
N_GROUPS = 8
TOPK_GROUPS = 4
EXPERTS_PER_GROUP = N_EXPERTS // N_GROUPS
ROUTED_SCALE = 2.5
N_MOD = 6
EPS = 1e-6

LANES = 128
BF16_SUBLANES = 16
HEAD_PAD = 2 * LANES

COL_LA = 0
COL_LC = 2048
COL_GB = 4096
COL_GC = 5120
COL_U = 6144
COL_CQ = 7168
COL_CKV = 7680
COL_KPE = 7936
N_PROJ = 8064

VMEM_LIMIT = 56 * 1024 * 1024

TM_IN = 1024
TN_IN = 1152
TM_QKV = 512
TQ_ATTN = 1024
HEADS_PER_STEP = 1
TK_ATTN = 512
TM_MERGE = 256
TM_ROUTE = 512
TM_EXP = 256
TM_TOK = 128
TN_ADA = 1024
DMA_UNROLL = 8


def _cparams(sem):
    return pltpu.CompilerParams(dimension_semantics=sem, vmem_limit_bytes=VMEM_LIMIT)


def _rms(x):
    return x * lax.rsqrt(jnp.mean(x * x, axis=-1, keepdims=True) + EPS)


def _silu(x):
    return x * jax.nn.sigmoid(x)


HI_HALF = 0xFFFF0000


def _pack_pair(lo, hi):
    lo_bits = lax.bitcast_convert_type(lo.astype(BF16).astype(F32), jnp.uint32) >> 16
    hi_bits = lax.bitcast_convert_type(hi.astype(BF16).astype(F32), jnp.uint32) & jnp.uint32(HI_HALF)
    return hi_bits | lo_bits


def _unpack_pair(u):
    lo = lax.bitcast_convert_type(u << 16, F32)
    hi = lax.bitcast_convert_type(u & jnp.uint32(HI_HALF), F32)
    return lo, hi


def _ada_kernel(c_ref, w_ref, b_ref, o_ref):
    ca = _silu(c_ref[...]).astype(BF16)
    o_ref[0] = jnp.dot(ca, w_ref[0].astype(BF16), preferred_element_type=F32) + b_ref[0]


def _ada_call(c8, w_ada, b_ada3):
    depth, d, n = w_ada.shape
    return pl.pallas_call(
        _ada_kernel,
        grid=(depth, n // TN_ADA),
        in_specs=[
            pl.BlockSpec((8, d), lambda l, j: (0, 0)),
            pl.BlockSpec((1, d, TN_ADA), lambda l, j: (l, 0, j)),
            pl.BlockSpec((1, 1, TN_ADA), lambda l, j: (l, 0, j)),
        ],
        out_specs=pl.BlockSpec((1, 8, TN_ADA), lambda l, j: (l, 0, j)),
        out_shape=jax.ShapeDtypeStruct((depth, 8, n), F32),
        compiler_params=_cparams(("parallel", "parallel")),
        name="ada_mod",
    )(c8, w_ada, b_ada3)


def _rope_kernel(pos_ref, inv_ref, cos_ref, sin_ref):
    ang = pos_ref[...].astype(F32) * inv_ref[...]
    lane = lax.broadcasted_iota(jnp.int32, ang.shape, 1)
    half = QK_ROPE // 2
    cos_ref[...] = jnp.where(lane < QK_ROPE, jnp.cos(ang), 0.0)
    s = jnp.sin(ang)
    sin_ref[...] = jnp.where(lane < half, -s, jnp.where(lane < QK_ROPE, s, 0.0))


def _rope_call(pos_col, inv128):
    t = pos_col.shape[0]
    tm = 1024
    return pl.pallas_call(
        _rope_kernel,
        grid=(t // tm,),
        in_specs=[pl.BlockSpec((tm, 1), lambda i: (i, 0)),
                  pl.BlockSpec((1, LANES), lambda i: (0, 0))],
        out_specs=[pl.BlockSpec((tm, LANES), lambda i: (i, 0)),
                   pl.BlockSpec((tm, LANES), lambda i: (i, 0))],
        out_shape=[jax.ShapeDtypeStruct((t, LANES), F32)] * 2,
        compiler_params=_cparams(("parallel",)),
        name="rope_tables",
    )(pos_col, inv128)


def _in_kernel(x_ref, g_ref, mod_ref, w_ref, o_ref, h_ref):
    @pl.when(pl.program_id(1) == 0)
    def _():
        m = mod_ref[0]
        h = _rms(x_ref[...]) * g_ref[...]
        h_ref[...] = (h * (1.0 + m[1:2]) + m[0:1]).astype(BF16)

    o_ref[...] = jnp.dot(h_ref[...], w_ref[...], preferred_element_type=F32).astype(o_ref.dtype)


def _in_call(x, g, mod, w_in_r, seq):
    t, d = x.shape
    tiles_per_seq = seq // TM_IN
    return pl.pallas_call(
        _in_kernel,
        grid=(t // TM_IN, N_PROJ // TN_IN),
        in_specs=[
            pl.BlockSpec((TM_IN, d), lambda i, j: (i, 0)),
            pl.BlockSpec((1, d), lambda i, j: (0, 0)),
            pl.BlockSpec((1, N_MOD, d), lambda i, j: (i // tiles_per_seq, 0, 0)),
            pl.BlockSpec((d, TN_IN), lambda i, j: (0, j)),
        ],
        out_specs=pl.BlockSpec((TM_IN, TN_IN), lambda i, j: (i, j)),
        out_shape=jax.ShapeDtypeStruct((t, N_PROJ), BF16),
        scratch_shapes=[pltpu.VMEM((TM_IN, d), BF16)],
        compiler_params=_cparams(("parallel", "arbitrary")),
        name="in_proj",
    )(x, g, mod, w_in_r)


def _rope128(t, cos, sin_signed):
    lane = lax.broadcasted_iota(jnp.int32, t.shape, 1)
    half = QK_ROPE // 2
    partner = jnp.where(lane < half, pltpu.roll(t, LANES - half, 1), pltpu.roll(t, half, 1))
    return t * cos + partner * sin_signed


def _qkv_kernel(cq_ref, ckv_ref, kpe_ref, cos_ref, sin_ref, gq_ref, gkv_ref,
                wq_ref, wk_ref, wv_ref, q_ref, k_ref, v_ref):
    cos = cos_ref[...]
    sin = sin_ref[...]
    qn = (_rms(cq_ref[...].astype(F32)) * gq_ref[...]).astype(BF16)
    q = jnp.dot(qn, wq_ref[...], preferred_element_type=F32) * (QK_HEAD ** -0.5)
    cn = (_rms(ckv_ref[...].astype(F32)) * gkv_ref[...]).astype(BF16)
    kn = jnp.dot(cn, wk_ref[...], preferred_element_type=F32)
    v_ref[...] = jnp.dot(cn, wv_ref[...], preferred_element_type=F32).astype(BF16)
    kr = _rope128(kpe_ref[...].astype(F32), cos, sin).astype(BF16)
    for h in range(N_HEADS):
        base = h * HEAD_PAD
        q_ref[:, base:base + LANES] = q[:, base:base + LANES].astype(BF16)
        q_ref[:, base + LANES:base + HEAD_PAD] = _rope128(
            q[:, base + LANES:base + HEAD_PAD], cos, sin).astype(BF16)
        k_ref[:, base:base + LANES] = kn[:, h * QK_NOPE:(h + 1) * QK_NOPE].astype(BF16)
        k_ref[:, base + LANES:base + HEAD_PAD] = kr


def _qkv_call(proj, cos, sin, gq, gkv, wq, wk, wv):
    t = proj.shape[0]
    tm = TM_QKV
    return pl.pallas_call(
        _qkv_kernel,
        grid=(t // tm,),
        in_specs=[
            pl.BlockSpec((tm, Q_LORA), lambda i: (i, COL_CQ // Q_LORA)),
            pl.BlockSpec((tm, KV_LORA), lambda i: (i, COL_CKV // KV_LORA)),
            pl.BlockSpec((tm, LANES), lambda i: (i, COL_KPE // LANES)),
            pl.BlockSpec((tm, LANES), lambda i: (i, 0)),
            pl.BlockSpec((tm, LANES), lambda i: (i, 0)),
            pl.BlockSpec((1, Q_LORA), lambda i: (0, 0)),
            pl.BlockSpec((1, KV_LORA), lambda i: (0, 0)),
            pl.BlockSpec(wq.shape, lambda i: (0, 0)),
            pl.BlockSpec(wk.shape, lambda i: (0, 0)),
            pl.BlockSpec(wv.shape, lambda i: (0, 0)),
        ],
        out_specs=[
            pl.BlockSpec((tm, N_HEADS * HEAD_PAD), lambda i: (i, 0)),
            pl.BlockSpec((tm, N_HEADS * HEAD_PAD), lambda i: (i, 0)),
            pl.BlockSpec((tm, N_HEADS * V_HEAD), lambda i: (i, 0)),
        ],
        out_shape=[
            jax.ShapeDtypeStruct((t, N_HEADS * HEAD_PAD), BF16),
            jax.ShapeDtypeStruct((t, N_HEADS * HEAD_PAD), BF16),
            jax.ShapeDtypeStruct((t, N_HEADS * V_HEAD), BF16),
        ],
        compiler_params=_cparams(("parallel",)),
        name="qkv_build",
    )(proj, proj, proj, cos, sin, gq, gkv, wq, wk, wv)


def _attn_kernel(q_ref, k_ref, v_ref, o_ref):
    tq = q_ref.shape[0]
    n_chunks = k_ref.shape[0] // TK_ATTN
    for hh in range(HEADS_PER_STEP):
        qk = slice(hh * HEAD_PAD, (hh + 1) * HEAD_PAD)
        vo = slice(hh * V_HEAD, (hh + 1) * V_HEAD)
        q = q_ref[:, qk]
        m = jnp.full((tq, 1), -jnp.inf, F32)
        l = jnp.zeros((tq, 1), F32)
        acc = jnp.zeros((tq, V_HEAD), F32)
        for j in range(n_chunks):
            rows = slice(j * TK_ATTN, (j + 1) * TK_ATTN)
            s = lax.dot_general(q, k_ref[rows, qk], (((1,), (1,)), ((), ())),
                                preferred_element_type=F32)
            m_new = jnp.maximum(m, jnp.max(s, axis=-1, keepdims=True))
            alpha = jnp.exp(m - m_new)
            p = jnp.exp(s - m_new)
            l = alpha * l + jnp.sum(p, axis=-1, keepdims=True)
            acc = alpha * acc + jnp.dot(p.astype(BF16), v_ref[rows, vo],
                                        preferred_element_type=F32)
            m = m_new
        o_ref[:, vo] = (acc / l).astype(o_ref.dtype)


def _attn_call(q, k, v, batch, seq):
    t = q.shape[0]
    nq = seq // TQ_ATTN
    hs = HEADS_PER_STEP
    return pl.pallas_call(
        _attn_kernel,
        grid=(batch, N_HEADS // hs, nq),
        in_specs=[
            pl.BlockSpec((TQ_ATTN, hs * HEAD_PAD), lambda b, h, i: (b * nq + i, h)),
            pl.BlockSpec((seq, hs * HEAD_PAD), lambda b, h, i: (b, h)),
            pl.BlockSpec((seq, hs * V_HEAD), lambda b, h, i: (b, h)),
        ],
        out_specs=pl.BlockSpec((TQ_ATTN, hs * V_HEAD), lambda b, h, i: (b * nq + i, h)),
        out_shape=jax.ShapeDtypeStruct((t, N_HEADS * V_HEAD), BF16),
        compiler_params=_cparams(("parallel", "parallel", "arbitrary")),
        name="mla_attention",
    )(q, k, v)


def _merge_kernel(o_ref, la_ref, lc_ref, gb_ref, gc_ref, u_ref, gcp_ref, up_ref, gcn_ref, un_ref,
                  x_ref, mod_ref, cw_ref, wo_ref, wco_ref, wout_ref, out_ref, *, tiles_per_seq):
    i = pl.program_id(0)
    tm = x_ref.shape[0]
    pos = i % tiles_per_seq
    has_prev = (pos != 0).astype(F32)
    has_next = (pos != tiles_per_seq - 1).astype(F32)
    z = gc_ref[...].astype(F32) * u_ref[...].astype(F32)
    last = BF16_SUBLANES - 1
    z_before = gcp_ref[last:last + 1, :].astype(F32) * up_ref[last:last + 1, :].astype(F32) * has_prev
    z_after = gcn_ref[0:1, :].astype(F32) * un_ref[0:1, :].astype(F32) * has_next
    row = lax.broadcasted_iota(jnp.int32, z.shape, 0)
    z_prev = jnp.where(row == 0, z_before, pltpu.roll(z, 1, 0))
    z_next = jnp.where(row == tm - 1, z_after, pltpu.roll(z, tm - 1, 0))
    cw = cw_ref[...]
    conv = cw[0:1] * z_prev + cw[1:2] * z + cw[2:3] * z_next
    y_conv = jnp.dot((gb_ref[...].astype(F32) * conv).astype(BF16), wco_ref[...],
                     preferred_element_type=F32)
    y_attn = jnp.dot(o_ref[...], wo_ref[...], preferred_element_type=F32)
    mix = (jax.nn.sigmoid(la_ref[...].astype(F32)) * y_attn
           + jax.nn.sigmoid(lc_ref[...].astype(F32)) * y_conv)
    y = jnp.dot(mix.astype(BF16), wout_ref[...], preferred_element_type=F32)
    out_ref[...] = x_ref[...] + mod_ref[0][2:3] * y


def _merge_call(o, proj, x, mod, conv_w, w_o, w_co, w_out, seq):
    t, d = x.shape
    tm = TM_MERGE
    tiles_per_seq = seq // tm
    hb = tm // BF16_SUBLANES
    n_halo = t // BF16_SUBLANES
    cw_blocks = CONV_WIDTH
    const = dict(pipeline_mode=pl.Buffered(1))
    prev_map = lambda c: (lambda i: (jnp.maximum(i * hb - 1, 0), c))
    next_map = lambda c: (lambda i: (jnp.minimum((i + 1) * hb, n_halo - 1), c))
    return pl.pallas_call(
        functools.partial(_merge_kernel, tiles_per_seq=tiles_per_seq),
        grid=(t // tm,),
        in_specs=[
            pl.BlockSpec((tm, d), lambda i: (i, 0)),
            pl.BlockSpec((tm, d), lambda i: (i, COL_LA // d)),
            pl.BlockSpec((tm, d), lambda i: (i, COL_LC // d)),
            pl.BlockSpec((tm, cw_blocks), lambda i: (i, COL_GB // cw_blocks)),
            pl.BlockSpec((tm, cw_blocks), lambda i: (i, COL_GC // cw_blocks)),
            pl.BlockSpec((tm, cw_blocks), lambda i: (i, COL_U // cw_blocks)),
            pl.BlockSpec((BF16_SUBLANES, cw_blocks), prev_map(COL_GC // cw_blocks)),
            pl.BlockSpec((BF16_SUBLANES, cw_blocks), prev_map(COL_U // cw_blocks)),
            pl.BlockSpec((BF16_SUBLANES, cw_blocks), next_map(COL_GC // cw_blocks)),
            pl.BlockSpec((BF16_SUBLANES, cw_blocks), next_map(COL_U // cw_blocks)),
            pl.BlockSpec((tm, d), lambda i: (i, 0)),
            pl.BlockSpec((1, N_MOD, d), lambda i: (i // tiles_per_seq, 0, 0)),
            pl.BlockSpec(conv_w.shape, lambda i: (0, 0)),
            pl.BlockSpec(w_o.shape, lambda i: (0, 0), **const),
            pl.BlockSpec(w_co.shape, lambda i: (0, 0), **const),
            pl.BlockSpec(w_out.shape, lambda i: (0, 0), **const),
        ],
        out_specs=pl.BlockSpec((tm, d), lambda i: (i, 0)),
        out_shape=jax.ShapeDtypeStruct((t, d), F32),
        compiler_params=_cparams(("parallel",)),
        name="conv_merge_out",
    )(o, proj, proj, proj, proj, proj, proj, proj, proj, proj, x, mod, conv_w, w_o, w_co, w_out)


def _first_max(vals, iota, sentinel):
    m = jnp.max(vals, axis=0, keepdims=True)
    first = jnp.min(jnp.where(vals == m, iota, sentinel), axis=0, keepdims=True)
    return m, first, iota == first


def _route_kernel(x_ref, g_ref, mod_ref, wr_ref, br_ref, w1_ref, w3_ref, w2_ref, tri_ref,
                  h_ref, xp_ref, idx_ref, wts_ref, rank_ref, cnt_ref, run_ref):
    @pl.when(pl.program_id(0) == 0)
    def _():
        run_ref[...] = jnp.zeros_like(run_ref)

    m = mod_ref[0]
    x = x_ref[...]
    h = (_rms(x) * g_ref[...]) * (1.0 + m[4:5]) + m[3:4]
    half = h.shape[1] // 2
    h_ref[...] = _pack_pair(h[:, :half], h[:, half:])
    hb = h.astype(BF16)
    a = jnp.dot(hb, w1_ref[...], preferred_element_type=F32)
    b = jnp.dot(hb, w3_ref[...], preferred_element_type=F32)
    shared = jnp.dot((_silu(a) * b).astype(BF16), w2_ref[...], preferred_element_type=F32)
    xp_ref[...] = x + m[5:6] * shared

    logits = lax.dot_general(wr_ref[...], h, (((1,), (1,)), ((), ())),
                             precision=lax.Precision.HIGHEST, preferred_element_type=F32)
    scores = jax.nn.sigmoid(logits)
    biased = scores + br_ref[...]
    tm = scores.shape[1]
    neg = -jnp.inf
    g = EXPERTS_PER_GROUP
    iota_g = lax.broadcasted_iota(jnp.int32, (g, tm), 0).astype(F32)
    group_rows = []
    for gi in range(N_GROUPS):
        blk = biased[gi * g:(gi + 1) * g, :]
        m1, _, hit = _first_max(blk, iota_g, g)
        m2 = jnp.max(jnp.where(hit, neg, blk), axis=0, keepdims=True)
        group_rows.append(m1 + m2)
    gs = jnp.concatenate(group_rows, axis=0)
    iota_n = lax.broadcasted_iota(jnp.int32, (N_GROUPS, tm), 0).astype(F32)
    chosen = jnp.zeros((N_GROUPS, tm), F32)
    for _ in range(TOPK_GROUPS):
        _, _, hit = _first_max(gs, iota_n, N_GROUPS)
        chosen = jnp.where(hit, 1.0, chosen)
        gs = jnp.where(hit, neg, gs)
    allowed = jnp.concatenate(
        [jnp.broadcast_to(chosen[gi:gi + 1, :], (g, tm)) for gi in range(N_GROUPS)], axis=0)
    cand = jnp.where(allowed > 0.0, biased, neg)
    iota_e = lax.broadcasted_iota(jnp.int32, (N_EXPERTS, tm), 0).astype(F32)

    run = run_ref[:, 0:1]
    tri = tri_ref[...]
    idx_rows, w_rows, rank_rows = [], [], []
    for _ in range(TOP_K):
        _, first, hit = _first_max(cand, iota_e, N_EXPERTS)
        idx_rows.append(first)
        w_rows.append(jnp.sum(jnp.where(hit, scores, 0.0), axis=0, keepdims=True))
        cand = jnp.where(hit, neg, cand)
        prefix = jnp.dot(jnp.where(hit, 1.0, 0.0).astype(BF16), tri, preferred_element_type=F32)
        rank_rows.append(jnp.sum(jnp.where(hit, run + prefix - 1.0, 0.0), axis=0, keepdims=True))
        run = run + prefix[:, tm - 1:tm]
    w = jnp.concatenate(w_rows, axis=0)
    idx_ref[...] = jnp.concatenate(idx_rows, axis=0).astype(jnp.int32)
    wts_ref[...] = w / jnp.sum(w, axis=0, keepdims=True) * ROUTED_SCALE
    rank_ref[...] = jnp.concatenate(rank_rows, axis=0).astype(jnp.int32)
    run_full = jnp.broadcast_to(run, run_ref.shape)
    run_ref[...] = run_full
    cnt_ref[...] = run_full


def _route_call(x, g, mod, wr_t, br_col, w1s, w3s, w2s, seq):
    t, d = x.shape
    tm = TM_ROUTE
    tiles_per_seq = seq // tm
    tri = (jnp.arange(tm)[:, None] <= jnp.arange(tm)[None, :]).astype(BF16)
    return pl.pallas_call(
        _route_kernel,
        grid=(t // tm,),
        in_specs=[
            pl.BlockSpec((tm, d), lambda i: (i, 0)),
            pl.BlockSpec((1, d), lambda i: (0, 0)),
            pl.BlockSpec((1, N_MOD, d), lambda i: (i // tiles_per_seq, 0, 0)),
            pl.BlockSpec(wr_t.shape, lambda i: (0, 0)),
            pl.BlockSpec(br_col.shape, lambda i: (0, 0)),
            pl.BlockSpec(w1s.shape, lambda i: (0, 0)),
            pl.BlockSpec(w3s.shape, lambda i: (0, 0)),
            pl.BlockSpec(w2s.shape, lambda i: (0, 0)),
            pl.BlockSpec((tm, tm), lambda i: (0, 0)),
        ],
        out_specs=[
            pl.BlockSpec((tm, d // 2), lambda i: (i, 0)),
            pl.BlockSpec((tm, d), lambda i: (i, 0)),
            pl.BlockSpec((TOP_K, tm), lambda i: (0, i)),
            pl.BlockSpec((TOP_K, tm), lambda i: (0, i)),
            pl.BlockSpec((TOP_K, tm), lambda i: (0, i)),
            pl.BlockSpec((N_EXPERTS, LANES), lambda i: (0, 0)),
        ],
        out_shape=[
            jax.ShapeDtypeStruct((t, d // 2), jnp.uint32),
            jax.ShapeDtypeStruct((t, d), F32),
            jax.ShapeDtypeStruct((TOP_K, t), jnp.int32),
            jax.ShapeDtypeStruct((TOP_K, t), F32),
            jax.ShapeDtypeStruct((TOP_K, t), jnp.int32),
            jax.ShapeDtypeStruct((N_EXPERTS, LANES), F32),
        ],
        scratch_shapes=[pltpu.VMEM((N_EXPERTS, LANES), F32)],
        compiler_params=_cparams(("arbitrary",)),
        name="moe_route",
    )(x, g, mod, wr_t, br_col, w1s, w3s, w2s, tri)


def _row_plan(idx_t, rank_t, cnt, n_tok):
    tm = TM_EXP
    n_tiles = TOP_K * n_tok // tm + N_EXPERTS
    counts = cnt[:, 0].astype(jnp.int32)
    padded = (counts + tm - 1) // tm * tm
    pend = jnp.cumsum(padded)
    pstart = pend - padded
    tile_start = jnp.arange(n_tiles, dtype=jnp.int32) * tm
    tile_e = jnp.minimum(jnp.sum((pend[None, :] <= tile_start[:, None]).astype(jnp.int32), axis=1),
                         N_EXPERTS - 1).astype(jnp.int32)
    nact = (pend[-1] // tm).astype(jnp.int32).reshape(1)
    zero_start = jnp.maximum(pend - tm, 0).astype(jnp.int32)
    experts = jnp.arange(N_EXPERTS, dtype=jnp.int32)[:, None, None]
    dest = rank_t + jnp.sum(jnp.where(idx_t[None] == experts, pstart[:, None, None], 0), axis=0)
    nt = n_tok // TM_TOK
    dest_tiles = dest.reshape(TOP_K, nt, TM_TOK).transpose(1, 0, 2).reshape(nt, 1, TOP_K * TM_TOK)
    return tile_e, nact, zero_start, dest_tiles.astype(jnp.int32), n_tiles * tm


def _dispatch_kernel(zs_ref, dest_ref, h_ref, xs_hbm, zbuf, zsem, sem):
    i = pl.program_id(0)
    tm = TM_TOK
    tme = zbuf.shape[0]

    def row_copy(j, dst_row):
        return pltpu.make_async_copy(h_ref.at[pl.ds(j, 1)], xs_hbm.at[pl.ds(dst_row, 1)], sem)

    @pl.when(i == 0)
    def _():
        zbuf[...] = jnp.zeros_like(zbuf)

        def zero_copy(e):
            start = pl.multiple_of(zs_ref[e], tme)
            return pltpu.make_async_copy(zbuf, xs_hbm.at[pl.ds(start, tme)], zsem)
        for e in range(N_EXPERTS):
            zero_copy(e).start()
        for e in range(N_EXPERTS):
            zero_copy(e).wait()

    def start_body(j, carry):
        for k in range(TOP_K):
            row_copy(j, dest_ref[0, 0, k * tm + j]).start()
        return carry
    lax.fori_loop(0, tm, start_body, 0, unroll=DMA_UNROLL)

    def wait_body(j, carry):
        for _ in range(TOP_K):
            row_copy(j, 0).wait()
        return carry
    lax.fori_loop(0, tm, wait_body, 0, unroll=DMA_UNROLL)


def _dispatch_call(zero_start, dest_tiles, hp, n_rows):
    t, dp = hp.shape
    nt = t // TM_TOK
    grid_spec = pltpu.PrefetchScalarGridSpec(
        num_scalar_prefetch=1,
        grid=(nt,),
        in_specs=[
            pl.BlockSpec((1, 1, TOP_K * TM_TOK), lambda i, zs: (i, 0, 0), memory_space=pltpu.SMEM),
            pl.BlockSpec((TM_TOK, dp), lambda i, zs: (i, 0)),
        ],
        out_specs=pl.BlockSpec(memory_space=pl.ANY),
        scratch_shapes=[
            pltpu.VMEM((TM_EXP, dp), hp.dtype),
            pltpu.SemaphoreType.DMA(()),
            pltpu.SemaphoreType.DMA(()),
        ],
    )
    return pl.pallas_call(
        _dispatch_kernel,
        grid_spec=grid_spec,
        out_shape=jax.ShapeDtypeStruct((n_rows, dp), hp.dtype),
        compiler_params=_cparams(("arbitrary",)),
        name="moe_dispatch",
    )(zero_start, dest_tiles, hp)


def _expert_kernel(te_ref, nact_ref, x_ref, w1_ref, w3_ref, w2_ref, y_ref, w1b, w3b, w2b):
    i = pl.program_id(0)

    @pl.when(i < nact_ref[0])
    def _():
        prev = jnp.maximum(i - 1, 0)

        @pl.when((i == 0) | (te_ref[i] != te_ref[prev]))
        def _():
            w1b[...] = w1_ref[0, 0].astype(BF16)
            w3b[...] = w3_ref[0, 0].astype(BF16)
            w2b[...] = w2_ref[0, 0].astype(BF16)

        half = x_ref.shape[1]
        x_lo, x_hi = _unpack_pair(x_ref[...])
        x_lo = x_lo.astype(BF16)
        x_hi = x_hi.astype(BF16)
        a = (jnp.dot(x_lo, w1b[:half, :], preferred_element_type=F32)
             + jnp.dot(x_hi, w1b[half:, :], preferred_element_type=F32))
        b = (jnp.dot(x_lo, w3b[:half, :], preferred_element_type=F32)
             + jnp.dot(x_hi, w3b[half:, :], preferred_element_type=F32))
        g = (_silu(a) * b).astype(BF16)
        y_ref[...] = _pack_pair(jnp.dot(g, w2b[:, :half], preferred_element_type=F32),
                                jnp.dot(g, w2b[:, half:], preferred_element_type=F32))


def _expert_call(layer, tile_e, nact, xs, w1_e, w3_e, w2_e):
    tm = TM_EXP
    n_rows, dp = xs.shape
    d = 2 * dp
    n_tiles = n_rows // tm
    de = w1_e.shape[-1]
    row_map = lambda i, te, na: (jnp.minimum(i, na[0] - 1), 0)
    w_map = lambda i, te, na: (layer, te[jnp.minimum(i, na[0] - 1)], 0, 0)
    grid_spec = pltpu.PrefetchScalarGridSpec(
        num_scalar_prefetch=2,
        grid=(n_tiles,),
        in_specs=[
            pl.BlockSpec((tm, dp), row_map),
            pl.BlockSpec((1, 1, d, de), w_map),
            pl.BlockSpec((1, 1, d, de), w_map),
            pl.BlockSpec((1, 1, de, d), w_map),
        ],
        out_specs=pl.BlockSpec((tm, dp), row_map),
        scratch_shapes=[
            pltpu.VMEM((d, de), BF16),
            pltpu.VMEM((d, de), BF16),
            pltpu.VMEM((de, d), BF16),
        ],
    )
    return pl.pallas_call(
        _expert_kernel,
        grid_spec=grid_spec,
        out_shape=jax.ShapeDtypeStruct((n_rows, dp), xs.dtype),
        compiler_params=_cparams(("arbitrary",)),
        name="moe_experts",
    )(tile_e, nact, xs, w1_e, w3_e, w2_e)


def _combine_kernel(dnext_ref, dfirst_ref, y_hbm, w_ref, xp_ref, mod_ref, gf_ref, out_ref,
                    ybuf, sem, *, final):
    i = pl.program_id(0)
    n = pl.num_programs(0)
    tm = TM_TOK
    slot = i % 2

    def row_copy(src_row, k, j, s):
        return pltpu.make_async_copy(y_hbm.at[pl.ds(src_row, 1)], ybuf.at[s, k, pl.ds(j, 1)],
                                     sem.at[s])

    def start_rows(d_ref, s):
        def body(j, carry):
            for k in range(TOP_K):
                row_copy(d_ref[0, 0, k * tm + j], k, j, s).start()
            return carry
        lax.fori_loop(0, tm, body, 0, unroll=DMA_UNROLL)

    def wait_rows(s):
        def body(j, carry):
            for k in range(TOP_K):
                row_copy(0, k, j, s).wait()
            return carry
        lax.fori_loop(0, tm, body, 0, unroll=DMA_UNROLL)

    @pl.when(i == 0)
    def _():
        start_rows(dfirst_ref, 0)

    for s in range(2):
        @pl.when((i + 1 < n) & (slot == 1 - s))
        def _():
            start_rows(dnext_ref, s)

    for s in range(2):
        @pl.when(slot == s)
        def _():
            wait_rows(s)
            w = w_ref[...]
            half = ybuf.shape[3]
            acc_lo, acc_hi = _unpack_pair(ybuf[s, 0])
            acc_lo = acc_lo * w[:, 0:1]
            acc_hi = acc_hi * w[:, 0:1]
            for k in range(1, TOP_K):
                y_lo, y_hi = _unpack_pair(ybuf[s, k])
                acc_lo = acc_lo + y_lo * w[:, k:k + 1]
                acc_hi = acc_hi + y_hi * w[:, k:k + 1]
            gate = mod_ref[0][5:6]
            x_lo = xp_ref[:, :half] + gate[:, :half] * acc_lo
            x_hi = xp_ref[:, half:] + gate[:, half:] * acc_hi
            if final:
                ms = (jnp.sum(x_lo * x_lo, axis=-1, keepdims=True)
                      + jnp.sum(x_hi * x_hi, axis=-1, keepdims=True)) / (2 * half)
                r = lax.rsqrt(ms + EPS)
                x_lo = x_lo * r * gf_ref[:, :half]
                x_hi = x_hi * r * gf_ref[:, half:]
            out_ref[:, :half] = x_lo
            out_ref[:, half:] = x_hi


def _combine_call(dest_tiles, ys, wts_col, xp, mod, g_final, seq, final):
    t, d = xp.shape
    tm = TM_TOK
    nt = t // tm
    tiles_per_seq = seq // tm
    return pl.pallas_call(
        functools.partial(_combine_kernel, final=final),
        grid=(nt,),
        in_specs=[
            pl.BlockSpec((1, 1, TOP_K * tm), lambda i: (jnp.minimum(i + 1, nt - 1), 0, 0),
                         memory_space=pltpu.SMEM),
            pl.BlockSpec((1, 1, TOP_K * tm), lambda i: (0, 0, 0), memory_space=pltpu.SMEM),
            pl.BlockSpec(memory_space=pl.ANY),
            pl.BlockSpec((tm, TOP_K), lambda i: (i, 0)),
            pl.BlockSpec((tm, d), lambda i: (i, 0)),
            pl.BlockSpec((1, N_MOD, d), lambda i: (i // tiles_per_seq, 0, 0)),
            pl.BlockSpec((1, d), lambda i: (0, 0)),
        ],
        out_specs=pl.BlockSpec((tm, d), lambda i: (i, 0)),
        out_shape=jax.ShapeDtypeStruct((t, d), F32),
        scratch_shapes=[
            pltpu.VMEM((2, TOP_K, tm, ys.shape[1]), ys.dtype),
            pltpu.SemaphoreType.DMA((2,)),
        ],
        compiler_params=_cparams(("arbitrary",)),
        name="moe_combine",
    )(dest_tiles, dest_tiles, ys, wts_col, xp, mod, g_final)


def _reorder_w_in(w):
    cq, ckv, kpe, gb, gc, u, la, lc = jnp.split(
        w, [512, 768, 832, 1856, 2880, 3904, 5952], axis=1)
    pad = jnp.zeros((w.shape[0], N_PROJ - 8000), w.dtype)
    return jnp.concatenate([la, lc, gb, gc, u, cq, ckv, kpe, pad], axis=1).astype(BF16)


def kernel(x, c, positions, w_ada, b_ada, g_mix, w_in, g_q, w_uq, g_kv, w_ukv, w_o_attn, conv_w,
           w_conv_out, w_out, g_ffn, w_router, b_router, w1_e, w3_e, w2_e, w1_s, w3_s, w2_s, g_final):
    batch, seq, d = x.shape
    depth = w_ada.shape[0]
    t = batch * seq
    xf = x.reshape(t, d)

    c8 = jnp.zeros((8, d), F32).at[:batch].set(c)
    mod_all = _ada_call(c8, w_ada, b_ada.reshape(depth, 1, N_MOD * d))

    inv = 1.0 / (ROPE_THETA ** (jnp.arange(0, QK_ROPE, 2, dtype=F32) / QK_ROPE))
    inv128 = jnp.concatenate([inv, inv, jnp.zeros((LANES - QK_ROPE,), F32)]).reshape(1, LANES)
    cos_t, sin_t = _rope_call(positions.reshape(t, 1), inv128)

    for l in range(depth):
        mod = mod_all[l, :batch].reshape(batch, N_MOD, d)
        wq = jnp.pad(w_uq[l], ((0, 0), (0, 0), (0, HEAD_PAD - QK_HEAD))).reshape(
            Q_LORA, N_HEADS * HEAD_PAD).astype(BF16)
        wk = w_ukv[l][:, :, :QK_NOPE].reshape(KV_LORA, N_HEADS * QK_NOPE).astype(BF16)
        wv = w_ukv[l][:, :, QK_NOPE:].reshape(KV_LORA, N_HEADS * V_HEAD).astype(BF16)

        proj = _in_call(xf, g_mix[l].reshape(1, d), mod, _reorder_w_in(w_in[l]), seq)
        q, k, v = _qkv_call(proj, cos_t, sin_t, g_q[l].reshape(1, Q_LORA),
                            g_kv[l].reshape(1, KV_LORA), wq, wk, wv)
        o = _attn_call(q, k, v, batch, seq)
        xf = _merge_call(o, proj, xf, mod, conv_w[l].reshape(3, CONV_WIDTH),
                         w_o_attn[l].astype(BF16), w_conv_out[l].astype(BF16),
                         w_out[l].astype(BF16), seq)

        h, xp, idx_t, wts_t, rank_t, cnt = _route_call(
            xf, g_ffn[l].reshape(1, d), mod, w_router[l].T, b_router[l].reshape(N_EXPERTS, 1),
            w1_s[l].astype(BF16), w3_s[l].astype(BF16), w2_s[l].astype(BF16), seq)
        tile_e, nact, zero_start, dest_tiles, n_rows = _row_plan(idx_t, rank_t, cnt, t)
        xs = _dispatch_call(zero_start, dest_tiles, h, n_rows)
        ys = _expert_call(l, tile_e, nact, xs, w1_e, w3_e, w2_e)
        xf = _combine_call(dest_tiles, ys, wts_t.T, xp, mod, g_final.reshape(1, d), seq,
                           final=(l == depth - 1))

    return xf.reshape(batch, seq, d)
```

```python
import functools

import jax
import jax.numpy as jnp
from jax import lax
from jax.experimental import pallas as pl
from jax.experimental.pallas import tpu as pltpu

F32 = jnp.float32
BF16 = jnp.bfloat16

N_HEADS = 16
Q_LORA = 512
KV_LORA = 256
QK_NOPE = 128
QK_ROPE = 64
V_HEAD = 128
QK_HEAD = QK_NOPE + QK_ROPE
ROPE_THETA = 10000.0
CONV_WIDTH = 1024
N_EXPERTS = 64
TOP_K = 8
N_GROUPS = 8
TOPK_GROUPS = 4
EXPERTS_PER_GROUP = N_EXPERTS // N_GROUPS
ROUTED_SCALE = 2.5
N_MOD = 6
EPS = 1e-6
LOG2_E = 1.4426950408889634

LANES = 128
BF16_SUBLANES = 16
HEAD_PAD = 2 * LANES

COL_LA = 0
COL_LC = 2048
COL_GB = 4096
COL_GC = 5120
COL_U = 6144
COL_CQ = 7168
COL_CKV = 7680
COL_KPE = 7936
N_PROJ = 8064

VMEM_LIMIT = 56 * 1024 * 1024

TM_IN = 1024
TN_IN = 1152
TM_QKV = 512
TQ_ATTN = 1024
HEADS_PER_STEP = 1
TK_ATTN = 512
TM_MERGE = 256
TM_ROUTE = 512
TM_EXP = 256
TM_TOK = 128
TN_ADA = 1024
DMA_UNROLL = 8


def _cparams(sem):
    return pltpu.CompilerParams(dimension_semantics=sem, vmem_limit_bytes=VMEM_LIMIT)


def _rms(x):
    return x * lax.rsqrt(jnp.mean(x * x, axis=-1, keepdims=True) + EPS)


def _silu(x):
    return x * jax.nn.sigmoid(x)


HI_HALF = 0xFFFF0000


def _pack_pair(lo, hi):
    lo_bits = lax.bitcast_convert_type(lo.astype(BF16).astype(F32), jnp.uint32) >> 16
    hi_bits = lax.bitcast_convert_type(hi.astype(BF16).astype(F32), jnp.uint32) & jnp.uint32(HI_HALF)
    return hi_bits | lo_bits


def _unpack_pair(u):
    lo = lax.bitcast_convert_type(u << 16, F32)
    hi = lax.bitcast_convert_type(u & jnp.uint32(HI_HALF), F32)
    return lo, hi


def _ada_kernel(c_ref, w_ref, b_ref, o_ref):
    ca = _silu(c_ref[...]).astype(BF16)
    o_ref[0] = jnp.dot(ca, w_ref[0].astype(BF16), preferred_element_type=F32) + b_ref[0]


def _ada_call(c8, w_ada, b_ada3):
    depth, d, n = w_ada.shape
    return pl.pallas_call(
        _ada_kernel,
        grid=(depth, n // TN_ADA),
        in_specs=[
            pl.BlockSpec((8, d), lambda l, j: (0, 0)),
            pl.BlockSpec((1, d, TN_ADA), lambda l, j: (l, 0, j)),
            pl.BlockSpec((1, 1, TN_ADA), lambda l, j: (l, 0, j)),
        ],
        out_specs=pl.BlockSpec((1, 8, TN_ADA), lambda l, j: (l, 0, j)),
        out_shape=jax.ShapeDtypeStruct((depth, 8, n), F32),
        compiler_params=_cparams(("parallel", "parallel")),
        name="ada_mod",
    )(c8, w_ada, b_ada3)


def _rope_kernel(pos_ref, inv_ref, cos_ref, sin_ref):
    ang = pos_ref[...].astype(F32) * inv_ref[...]
    lane = lax.broadcasted_iota(jnp.int32, ang.shape, 1)
    half = QK_ROPE // 2
    cos_ref[...] = jnp.where(lane < QK_ROPE, jnp.cos(ang), 0.0)
    s = jnp.sin(ang)
    sin_ref[...] = jnp.where(lane < half, -s, jnp.where(lane < QK_ROPE, s, 0.0))


def _rope_call(pos_col, inv128):
    t = pos_col.shape[0]
    tm = 1024
    return pl.pallas_call(
        _rope_kernel,
        grid=(t // tm,),
        in_specs=[pl.BlockSpec((tm, 1), lambda i: (i, 0)),
                  pl.BlockSpec((1, LANES), lambda i: (0, 0))],
        out_specs=[pl.BlockSpec((tm, LANES), lambda i: (i, 0)),
                   pl.BlockSpec((tm, LANES), lambda i: (i, 0))],
        out_shape=[jax.ShapeDtypeStruct((t, LANES), F32)] * 2,
        compiler_params=_cparams(("parallel",)),
        name="rope_tables",
    )(pos_col, inv128)


def _in_kernel(x_ref, g_ref, mod_ref, w_ref, o_ref, h_ref):
    @pl.when(pl.program_id(1) == 0)
    def _():
        m = mod_ref[0]
        h = _rms(x_ref[...]) * g_ref[...]
        h_ref[...] = (h * (1.0 + m[1:2]) + m[0:1]).astype(BF16)

    o_ref[...] = jnp.dot(h_ref[...], w_ref[...], preferred_element_type=F32).astype(o_ref.dtype)


def _in_call(x, g, mod, w_in_r, seq):
    t, d = x.shape
    tiles_per_seq = seq // TM_IN
    return pl.pallas_call(
        _in_kernel,
        grid=(t // TM_IN, N_PROJ // TN_IN),
        in_specs=[
            pl.BlockSpec((TM_IN, d), lambda i, j: (i, 0)),
            pl.BlockSpec((1, d), lambda i, j: (0, 0)),
            pl.BlockSpec((1, N_MOD, d), lambda i, j: (i // tiles_per_seq, 0, 0)),
            pl.BlockSpec((d, TN_IN), lambda i, j: (0, j)),
        ],
        out_specs=pl.BlockSpec((TM_IN, TN_IN), lambda i, j: (i, j)),
        out_shape=jax.ShapeDtypeStruct((t, N_PROJ), BF16),
        scratch_shapes=[pltpu.VMEM((TM_IN, d), BF16)],
        compiler_params=_cparams(("parallel", "arbitrary")),
        name="in_proj",
    )(x, g, mod, w_in_r)


def _rope128(t, cos, sin_signed):
    lane = lax.broadcasted_iota(jnp.int32, t.shape, 1)
    half = QK_ROPE // 2
    partner = jnp.where(lane < half, pltpu.roll(t, LANES - half, 1), pltpu.roll(t, half, 1))
    return t * cos + partner * sin_signed


def _qkv_kernel(cq_ref, ckv_ref, kpe_ref, cos_ref, sin_ref, gq_ref, gkv_ref,
                wq_ref, wk_ref, wv_ref, q_ref, k_ref, v_ref):
    cos = cos_ref[...]
    sin = sin_ref[...]
    qn = (_rms(cq_ref[...].astype(F32)) * gq_ref[...]).astype(BF16)
    q = jnp.dot(qn, wq_ref[...], preferred_element_type=F32) * (QK_HEAD ** -0.5 * LOG2_E)
    cn = (_rms(ckv_ref[...].astype(F32)) * gkv_ref[...]).astype(BF16)
    kn = jnp.dot(cn, wk_ref[...], preferred_element_type=F32)
    v_ref[...] = jnp.dot(cn, wv_ref[...], preferred_element_type=F32).astype(BF16)
    kr = _rope128(kpe_ref[...].astype(F32), cos, sin).astype(BF16)
    for h in range(N_HEADS):
        base = h * HEAD_PAD
        q_ref[:, base:base + LANES] = q[:, base:base + LANES].astype(BF16)
        q_ref[:, base + LANES:base + HEAD_PAD] = _rope128(
            q[:, base + LANES:base + HEAD_PAD], cos, sin).astype(BF16)
        k_ref[:, base:base + LANES] = kn[:, h * QK_NOPE:(h + 1) * QK_NOPE].astype(BF16)
        k_ref[:, base + LANES:base + HEAD_PAD] = kr


def _qkv_call(proj, cos, sin, gq, gkv, wq, wk, wv):
    t = proj.shape[0]
    tm = TM_QKV
    return pl.pallas_call(
        _qkv_kernel,
        grid=(t // tm,),
        in_specs=[
            pl.BlockSpec((tm, Q_LORA), lambda i: (i, COL_CQ // Q_LORA)),
            pl.BlockSpec((tm, KV_LORA), lambda i: (i, COL_CKV // KV_LORA)),
            pl.BlockSpec((tm, LANES), lambda i: (i, COL_KPE // LANES)),
            pl.BlockSpec((tm, LANES), lambda i: (i, 0)),
            pl.BlockSpec((tm, LANES), lambda i: (i, 0)),
            pl.BlockSpec((1, Q_LORA), lambda i: (0, 0)),
            pl.BlockSpec((1, KV_LORA), lambda i: (0, 0)),
            pl.BlockSpec(wq.shape, lambda i: (0, 0)),
            pl.BlockSpec(wk.shape, lambda i: (0, 0)),
            pl.BlockSpec(wv.shape, lambda i: (0, 0)),
        ],
        out_specs=[
            pl.BlockSpec((tm, N_HEADS * HEAD_PAD), lambda i: (i, 0)),
            pl.BlockSpec((tm, N_HEADS * HEAD_PAD), lambda i: (i, 0)),
            pl.BlockSpec((tm, N_HEADS * V_HEAD), lambda i: (i, 0)),
        ],
        out_shape=[
            jax.ShapeDtypeStruct((t, N_HEADS * HEAD_PAD), BF16),
            jax.ShapeDtypeStruct((t, N_HEADS * HEAD_PAD), BF16),
            jax.ShapeDtypeStruct((t, N_HEADS * V_HEAD), BF16),
        ],
        compiler_params=_cparams(("parallel",)),
        name="qkv_build",
    )(proj, proj, proj, cos, sin, gq, gkv, wq, wk, wv)


def _attn_kernel(q_ref, k_ref, v_ref, o_ref):
    tq = q_ref.shape[0]
    n_chunks = k_ref.shape[0] // TK_ATTN
    for hh in range(HEADS_PER_STEP):
        qk = slice(hh * HEAD_PAD, (hh + 1) * HEAD_PAD)
        vo = slice(hh * V_HEAD, (hh + 1) * V_HEAD)
        q = q_ref[:, qk]
        m = jnp.full((tq, 1), -jnp.inf, F32)
        l = jnp.zeros((tq, 1), F32)
        acc = jnp.zeros((tq, V_HEAD), F32)
        for j in range(n_chunks):
            rows = slice(j * TK_ATTN, (j + 1) * TK_ATTN)
            s = lax.dot_general(q, k_ref[rows, qk], (((1,), (1,)), ((), ())),
                                preferred_element_type=F32)
            m_new = jnp.maximum(m, jnp.max(s, axis=-1, keepdims=True))
            alpha = jnp.exp2(m - m_new)
            p = jnp.exp2(s - m_new)
            l = alpha * l + jnp.sum(p, axis=-1, keepdims=True)
            acc = alpha * acc + jnp.dot(p.astype(BF16), v_ref[rows, vo],
                                        preferred_element_type=F32)
            m = m_new
        o_ref[:, vo] = (acc / l).astype(o_ref.dtype)


def _attn_call(q, k, v, batch, seq):
    t = q.shape[0]
    nq = seq // TQ_ATTN
    hs = HEADS_PER_STEP
    return pl.pallas_call(
        _attn_kernel,
        grid=(batch, N_HEADS // hs, nq),
        in_specs=[
            pl.BlockSpec((TQ_ATTN, hs * HEAD_PAD), lambda b, h, i: (b * nq + i, h)),
            pl.BlockSpec((seq, hs * HEAD_PAD), lambda b, h, i: (b, h)),
            pl.BlockSpec((seq, hs * V_HEAD), lambda b, h, i: (b, h)),
        ],
        out_specs=pl.BlockSpec((TQ_ATTN, hs * V_HEAD), lambda b, h, i: (b * nq + i, h)),
        out_shape=jax.ShapeDtypeStruct((t, N_HEADS * V_HEAD), BF16),
        compiler_params=_cparams(("parallel", "parallel", "arbitrary")),
        name="mla_attention",
    )(q, k, v)


def _merge_kernel(o_ref, la_ref, lc_ref, gb_ref, gc_ref, u_ref, gcp_ref, up_ref, gcn_ref, un_ref,
                  x_ref, mod_ref, cw_ref, wo_ref, wco_ref, wout_ref, out_ref, *, tiles_per_seq):
    i = pl.program_id(0)
    tm = x_ref.shape[0]
    pos = i % tiles_per_seq
    has_prev = (pos != 0).astype(F32)
    has_next = (pos != tiles_per_seq - 1).astype(F32)
    z = gc_ref[...].astype(F32) * u_ref[...].astype(F32)
    last = BF16_SUBLANES - 1
    z_before = gcp_ref[last:last + 1, :].astype(F32) * up_ref[last:last + 1, :].astype(F32) * has_prev
    z_after = gcn_ref[0:1, :].astype(F32) * un_ref[0:1, :].astype(F32) * has_next
    row = lax.broadcasted_iota(jnp.int32, z.shape, 0)
    z_prev = jnp.where(row == 0, z_before, pltpu.roll(z, 1, 0))
    z_next = jnp.where(row == tm - 1, z_after, pltpu.roll(z, tm - 1, 0))
    cw = cw_ref[...]
    conv = cw[0:1] * z_prev + cw[1:2] * z + cw[2:3] * z_next
    y_conv = jnp.dot((gb_ref[...].astype(F32) * conv).astype(BF16), wco_ref[...],
                     preferred_element_type=F32)
    y_attn = jnp.dot(o_ref[...], wo_ref[...], preferred_element_type=F32)
    mix = (jax.nn.sigmoid(la_ref[...].astype(F32)) * y_attn
           + jax.nn.sigmoid(lc_ref[...].astype(F32)) * y_conv)
    y = jnp.dot(mix.astype(BF16), wout_ref[...], preferred_element_type=F32)
    out_ref[...] = x_ref[...] + mod_ref[0][2:3] * y


def _merge_call(o, proj, x, mod, conv_w, w_o, w_co, w_out, seq):
    t, d = x.shape
    tm = TM_MERGE
    tiles_per_seq = seq // tm
    hb = tm // BF16_SUBLANES
    n_halo = t // BF16_SUBLANES
    cw_blocks = CONV_WIDTH
    const = dict(pipeline_mode=pl.Buffered(1))
    prev_map = lambda c: (lambda i: (jnp.maximum(i * hb - 1, 0), c))
    next_map = lambda c: (lambda i: (jnp.minimum((i + 1) * hb, n_halo - 1), c))
    return pl.pallas_call(
        functools.partial(_merge_kernel, tiles_per_seq=tiles_per_seq),
        grid=(t // tm,),
        in_specs=[
            pl.BlockSpec((tm, d), lambda i: (i, 0)),
            pl.BlockSpec((tm, d), lambda i: (i, COL_LA // d)),
            pl.BlockSpec((tm, d), lambda i: (i, COL_LC // d)),
            pl.BlockSpec((tm, cw_blocks), lambda i: (i, COL_GB // cw_blocks)),
            pl.BlockSpec((tm, cw_blocks), lambda i: (i, COL_GC // cw_blocks)),
            pl.BlockSpec((tm, cw_blocks), lambda i: (i, COL_U // cw_blocks)),
            pl.BlockSpec((BF16_SUBLANES, cw_blocks), prev_map(COL_GC // cw_blocks)),
            pl.BlockSpec((BF16_SUBLANES, cw_blocks), prev_map(COL_U // cw_blocks)),
            pl.BlockSpec((BF16_SUBLANES, cw_blocks), next_map(COL_GC // cw_blocks)),
            pl.BlockSpec((BF16_SUBLANES, cw_blocks), next_map(COL_U // cw_blocks)),
            pl.BlockSpec((tm, d), lambda i: (i, 0)),
            pl.BlockSpec((1, N_MOD, d), lambda i: (i // tiles_per_seq, 0, 0)),
            pl.BlockSpec(conv_w.shape, lambda i: (0, 0)),
            pl.BlockSpec(w_o.shape, lambda i: (0, 0), **const),
            pl.BlockSpec(w_co.shape, lambda i: (0, 0), **const),
            pl.BlockSpec(w_out.shape, lambda i: (0, 0), **const),
        ],
        out_specs=pl.BlockSpec((tm, d), lambda i: (i, 0)),
        out_shape=jax.ShapeDtypeStruct((t, d), F32),
        compiler_params=_cparams(("parallel",)),
        name="conv_merge_out",
    )(o, proj, proj, proj, proj, proj, proj, proj, proj, proj, x, mod, conv_w, w_o, w_co, w_out)


def _first_max(vals, iota, sentinel):
    m = jnp.max(vals, axis=0, keepdims=True)
    first = jnp.min(jnp.where(vals == m, iota, sentinel), axis=0, keepdims=True)
    return m, first, iota == first


def _route_kernel(x_ref, g_ref, mod_ref, wr_ref, br_ref, w1_ref, w3_ref, w2_ref, tri_ref,
                  h_ref, xp_ref, idx_ref, wts_ref, rank_ref, cnt_ref, run_ref):
    @pl.when(pl.program_id(0) == 0)
    def _():
        run_ref[...] = jnp.zeros_like(run_ref)

    m = mod_ref[0]
    x = x_ref[...]
    h = (_rms(x) * g_ref[...]) * (1.0 + m[4:5]) + m[3:4]
    half = h.shape[1] // 2
    h_ref[...] = _pack_pair(h[:, :half], h[:, half:])
    hb = h.astype(BF16)
    a = jnp.dot(hb, w1_ref[...], preferred_element_type=F32)
    b = jnp.dot(hb, w3_ref[...], preferred_element_type=F32)
    shared = jnp.dot((_silu(a) * b).astype(BF16), w2_ref[...], preferred_element_type=F32)
    xp_ref[...] = x + m[5:6] * shared

    logits = lax.dot_general(wr_ref[...], h, (((1,), (1,)), ((), ())),
                             precision=lax.Precision.HIGHEST, preferred_element_type=F32)
    scores = jax.nn.sigmoid(logits)
    biased = scores + br_ref[...]
    tm = scores.shape[1]
    neg = -jnp.inf
    g = EXPERTS_PER_GROUP
    iota_g = lax.broadcasted_iota(jnp.int32, (g, tm), 0).astype(F32)
    group_rows = []
    for gi in range(N_GROUPS):
        blk = biased[gi * g:(gi + 1) * g, :]
        m1, _, hit = _first_max(blk, iota_g, g)
        m2 = jnp.max(jnp.where(hit, neg, blk), axis=0, keepdims=True)
        group_rows.append(m1 + m2)
    gs = jnp.concatenate(group_rows, axis=0)
    iota_n = lax.broadcasted_iota(jnp.int32, (N_GROUPS, tm), 0).astype(F32)
    chosen = jnp.zeros((N_GROUPS, tm), F32)
    for _ in range(TOPK_GROUPS):
        _, _, hit = _first_max(gs, iota_n, N_GROUPS)
        chosen = jnp.where(hit, 1.0, chosen)
        gs = jnp.where(hit, neg, gs)
    allowed = jnp.concatenate(
        [jnp.broadcast_to(chosen[gi:gi + 1, :], (g, tm)) for gi in range(N_GROUPS)], axis=0)
    cand = jnp.where(allowed > 0.0, biased, neg)
    iota_e = lax.broadcasted_iota(jnp.int32, (N_EXPERTS, tm), 0).astype(F32)

    run = run_ref[:, 0:1]
    tri = tri_ref[...]
    idx_rows, w_rows, rank_rows = [], [], []
    for _ in range(TOP_K):
        _, first, hit = _first_max(cand, iota_e, N_EXPERTS)
        idx_rows.append(first)
        w_rows.append(jnp.sum(jnp.where(hit, scores, 0.0), axis=0, keepdims=True))
        cand = jnp.where(hit, neg, cand)
        prefix = jnp.dot(jnp.where(hit, 1.0, 0.0).astype(BF16), tri, preferred_element_type=F32)
        rank_rows.append(jnp.sum(jnp.where(hit, run + prefix - 1.0, 0.0), axis=0, keepdims=True))
        run = run + prefix[:, tm - 1:tm]
    w = jnp.concatenate(w_rows, axis=0)
    idx_ref[...] = jnp.concatenate(idx_rows, axis=0).astype(jnp.int32)
    wts_ref[...] = w / jnp.sum(w, axis=0, keepdims=True) * ROUTED_SCALE
    rank_ref[...] = jnp.concatenate(rank_rows, axis=0).astype(jnp.int32)
    run_full = jnp.broadcast_to(run, run_ref.shape)
    run_ref[...] = run_full
    cnt_ref[...] = run_full


def _route_call(x, g, mod, wr_t, br_col, w1s, w3s, w2s, seq):
    t, d = x.shape
    tm = TM_ROUTE
    tiles_per_seq = seq // tm
    tri = (jnp.arange(tm)[:, None] <= jnp.arange(tm)[None, :]).astype(BF16)
    return pl.pallas_call(
        _route_kernel,
        grid=(t // tm,),
        in_specs=[
            pl.BlockSpec((tm, d), lambda i: (i, 0)),
            pl.BlockSpec((1, d), lambda i: (0, 0)),
            pl.BlockSpec((1, N_MOD, d), lambda i: (i // tiles_per_seq, 0, 0)),
            pl.BlockSpec(wr_t.shape, lambda i: (0, 0)),
            pl.BlockSpec(br_col.shape, lambda i: (0, 0)),
            pl.BlockSpec(w1s.shape, lambda i: (0, 0)),
            pl.BlockSpec(w3s.shape, lambda i: (0, 0)),
            pl.BlockSpec(w2s.shape, lambda i: (0, 0)),
            pl.BlockSpec((tm, tm), lambda i: (0, 0)),
        ],
        out_specs=[
            pl.BlockSpec((tm, d // 2), lambda i: (i, 0)),
            pl.BlockSpec((tm, d), lambda i: (i, 0)),
            pl.BlockSpec((TOP_K, tm), lambda i: (0, i)),
            pl.BlockSpec((TOP_K, tm), lambda i: (0, i)),
            pl.BlockSpec((TOP_K, tm), lambda i: (0, i)),
            pl.BlockSpec((N_EXPERTS, LANES), lambda i: (0, 0)),
        ],
        out_shape=[
            jax.ShapeDtypeStruct((t, d // 2), jnp.uint32),
            jax.ShapeDtypeStruct((t, d), F32),
            jax.ShapeDtypeStruct((TOP_K, t), jnp.int32),
            jax.ShapeDtypeStruct((TOP_K, t), F32),
            jax.ShapeDtypeStruct((TOP_K, t), jnp.int32),
            jax.ShapeDtypeStruct((N_EXPERTS, LANES), F32),
        ],
        scratch_shapes=[pltpu.VMEM((N_EXPERTS, LANES), F32)],
        compiler_params=_cparams(("arbitrary",)),
        name="moe_route",
    )(x, g, mod, wr_t, br_col, w1s, w3s, w2s, tri)


def _row_plan(idx_t, rank_t, cnt, n_tok):
    tm = TM_EXP
    n_tiles = TOP_K * n_tok // tm + N_EXPERTS
    counts = cnt[:, 0].astype(jnp.int32)
    padded = (counts + tm - 1) // tm * tm
    pend = jnp.cumsum(padded)
    pstart = pend - padded
    tile_start = jnp.arange(n_tiles, dtype=jnp.int32) * tm
    tile_e = jnp.minimum(jnp.sum((pend[None, :] <= tile_start[:, None]).astype(jnp.int32), axis=1),
                         N_EXPERTS - 1).astype(jnp.int32)
    nact = (pend[-1] // tm).astype(jnp.int32).reshape(1)
    eids = jnp.arange(N_EXPERTS, dtype=jnp.int32)
    nonempty = counts > 0
    ordinal = jnp.cumsum(nonempty.astype(jnp.int32)) - 1
    later = lax.cummin(jnp.where(nonempty, eids, N_EXPERTS)[::-1])[::-1]
    next_e = jnp.concatenate([later[1:], jnp.full((1,), N_EXPERTS, jnp.int32)])
    is_first = (tile_start == pstart[tile_e]).astype(jnp.int32)
    tile_info = jnp.stack([tile_e, is_first, ordinal[tile_e] % 2, next_e[tile_e]], axis=0)
    zero_start = jnp.maximum(pend - tm, 0).astype(jnp.int32)
    experts = jnp.arange(N_EXPERTS, dtype=jnp.int32)[:, None, None]
    dest = rank_t + jnp.sum(jnp.where(idx_t[None] == experts, pstart[:, None, None], 0), axis=0)
    nt = n_tok // TM_TOK
    dest_tiles = dest.reshape(TOP_K, nt, TM_TOK).transpose(1, 0, 2).reshape(nt, 1, TOP_K * TM_TOK)
    return tile_info.astype(jnp.int32), nact, zero_start, dest_tiles.astype(jnp.int32), n_tiles * tm


def _dispatch_kernel(zs_ref, dest_ref, h_ref, xs_hbm, zbuf, zsem, sem):
    i = pl.program_id(0)
    tm = TM_TOK
    tme = zbuf.shape[0]

    def row_copy(j, dst_row):
        return pltpu.make_async_copy(h_ref.at[pl.ds(j, 1)], xs_hbm.at[pl.ds(dst_row, 1)], sem)

    @pl.when(i == 0)
    def _():
        zbuf[...] = jnp.zeros_like(zbuf)

        def zero_copy(e):
            start = pl.multiple_of(zs_ref[e], tme)
            return pltpu.make_async_copy(zbuf, xs_hbm.at[pl.ds(start, tme)], zsem)
        for e in range(N_EXPERTS):
            zero_copy(e).start()
        for e in range(N_EXPERTS):
            zero_copy(e).wait()

    def start_body(j, carry):
        for k in range(TOP_K):
            row_copy(j, dest_ref[0, 0, k * tm + j]).start(priority=k % 2)
        return carry
    lax.fori_loop(0, tm, start_body, 0, unroll=DMA_UNROLL)

    def wait_body(j, carry):
        for _ in range(TOP_K):
            row_copy(j, 0).wait()
        return carry
    lax.fori_loop(0, tm, wait_body, 0, unroll=DMA_UNROLL)


def _dispatch_call(zero_start, dest_tiles, hp, n_rows):
    t, dp = hp.shape
    nt = t // TM_TOK
    grid_spec = pltpu.PrefetchScalarGridSpec(
        num_scalar_prefetch=1,
        grid=(nt,),
        in_specs=[
            pl.BlockSpec((1, 1, TOP_K * TM_TOK), lambda i, zs: (i, 0, 0), memory_space=pltpu.SMEM),
            pl.BlockSpec((TM_TOK, dp), lambda i, zs: (i, 0)),
        ],
        out_specs=pl.BlockSpec(memory_space=pl.ANY),
        scratch_shapes=[
            pltpu.VMEM((TM_EXP, dp), hp.dtype),
            pltpu.SemaphoreType.DMA(()),
            pltpu.SemaphoreType.DMA(()),
        ],
    )
    return pl.pallas_call(
        _dispatch_kernel,
        grid_spec=grid_spec,
        out_shape=jax.ShapeDtypeStruct((n_rows, dp), hp.dtype),
        compiler_params=_cparams(("arbitrary",)),
        name="moe_dispatch",
    )(zero_start, dest_tiles, hp)


def _expert_kernel(ti_ref, nact_ref, x_ref, w1_hbm, w3_hbm, w2_hbm, y_ref,
                   w1f, w3f, w2f, w1b, w3b, w2b, wsem, *, layer):
    i = pl.program_id(0)

    def weight_copies(e, s):
        return (pltpu.make_async_copy(w1_hbm.at[layer, e], w1f.at[s], wsem.at[s]),
                pltpu.make_async_copy(w3_hbm.at[layer, e], w3f.at[s], wsem.at[s]),
                pltpu.make_async_copy(w2_hbm.at[layer, e], w2f.at[s], wsem.at[s]))

    @pl.when(i < nact_ref[0])
    def _():
        @pl.when(ti_ref[1, i] == 1)
        def _():
            e = ti_ref[0, i]
            nxt = ti_ref[3, i]
            for s in range(2):
                @pl.when(ti_ref[2, i] == s)
                def _():
                    @pl.when(i == 0)
                    def _():
                        for c in weight_copies(e, s):
                            c.start()
                    for c in weight_copies(e, s):
                        c.wait()

                    @pl.when(nxt < N_EXPERTS)
                    def _():
                        for c in weight_copies(nxt, 1 - s):
                            c.start()
                    w1b[...] = w1f[s].astype(BF16)
                    w3b[...] = w3f[s].astype(BF16)
                    w2b[...] = w2f[s].astype(BF16)

        half = x_ref.shape[1]
        x_lo, x_hi = _unpack_pair(x_ref[...])
        x_lo = x_lo.astype(BF16)
        x_hi = x_hi.astype(BF16)
        a = (jnp.dot(x_lo, w1b[:half, :], preferred_element_type=F32)
             + jnp.dot(x_hi, w1b[half:, :], preferred_element_type=F32))
        b = (jnp.dot(x_lo, w3b[:half, :], preferred_element_type=F32)
             + jnp.dot(x_hi, w3b[half:, :], preferred_element_type=F32))
        g = (_silu(a) * b).astype(BF16)
        y_ref[...] = _pack_pair(jnp.dot(g, w2b[:, :half], preferred_element_type=F32),
                                jnp.dot(g, w2b[:, half:], preferred_element_type=F32))


def _expert_call(layer, tile_info, nact, xs, w1_e, w3_e, w2_e):
    tm = TM_EXP
    n_rows, dp = xs.shape
    d = 2 * dp
    n_tiles = n_rows // tm
    de = w1_e.shape[-1]
    row_map = lambda i, ti, na: (jnp.minimum(i, na[0] - 1), 0)
    grid_spec = pltpu.PrefetchScalarGridSpec(
        num_scalar_prefetch=2,
        grid=(n_tiles,),
        in_specs=[
            pl.BlockSpec((tm, dp), row_map),
            pl.BlockSpec(memory_space=pl.ANY),
            pl.BlockSpec(memory_space=pl.ANY),
            pl.BlockSpec(memory_space=pl.ANY),
        ],
        out_specs=pl.BlockSpec((tm, dp), row_map),
        scratch_shapes=[
            pltpu.VMEM((2, d, de), F32),
            pltpu.VMEM((2, d, de), F32),
            pltpu.VMEM((2, de, d), F32),
            pltpu.VMEM((d, de), BF16),
            pltpu.VMEM((d, de), BF16),
            pltpu.VMEM((de, d), BF16),
            pltpu.SemaphoreType.DMA((2,)),
        ],
    )
    return pl.pallas_call(
        functools.partial(_expert_kernel, layer=layer),
        grid_spec=grid_spec,
        out_shape=jax.ShapeDtypeStruct((n_rows, dp), xs.dtype),
        compiler_params=_cparams(("arbitrary",)),
        name="moe_experts",
    )(tile_info, nact, xs, w1_e, w3_e, w2_e)


def _combine_kernel(dnext_ref, dfirst_ref, y_hbm, w_ref, xp_ref, mod_ref, gf_ref, out_ref,
                    ybuf, sem, *, final):
    i = pl.program_id(0)
    n = pl.num_programs(0)
    tm = TM_TOK
    slot = i % 2

    def row_copy(src_row, k, j, s):
        return pltpu.make_async_copy(y_hbm.at[pl.ds(src_row, 1)], ybuf.at[s, k, pl.ds(j, 1)],
                                     sem.at[s])

    def start_rows(d_ref, s):
        def body(j, carry):
            for k in range(TOP_K):
                row_copy(d_ref[0, 0, k * tm + j], k, j, s).start(priority=k % 2)
            return carry
        lax.fori_loop(0, tm, body, 0, unroll=DMA_UNROLL)

    def wait_rows(s):
        def body(j, carry):
            for k in range(TOP_K):
                row_copy(0, k, j, s).wait()
            return carry
        lax.fori_loop(0, tm, body, 0, unroll=DMA_UNROLL)

    @pl.when(i == 0)
    def _():
        start_rows(dfirst_ref, 0)

    for s in range(2):
        @pl.when((i + 1 < n) & (slot == 1 - s))
        def _():
            start_rows(dnext_ref, s)

    for s in range(2):
        @pl.when(slot == s)
        def _():
            wait_rows(s)
            w = w_ref[...]
            half = ybuf.shape[3]
            acc_lo, acc_hi = _unpack_pair(ybuf[s, 0])
            acc_lo = acc_lo * w[:, 0:1]
            acc_hi = acc_hi * w[:, 0:1]
            for k in range(1, TOP_K):
                y_lo, y_hi = _unpack_pair(ybuf[s, k])
                acc_lo = acc_lo + y_lo * w[:, k:k + 1]
                acc_hi = acc_hi + y_hi * w[:, k:k + 1]
            gate = mod_ref[0][5:6]
            x_lo = xp_ref[:, :half] + gate[:, :half] * acc_lo
            x_hi = xp_ref[:, half:] + gate[:, half:] * acc_hi
            if final:
                ms = (jnp.sum(x_lo * x_lo, axis=-1, keepdims=True)
                      + jnp.sum(x_hi * x_hi, axis=-1, keepdims=True)) / (2 * half)
                r = lax.rsqrt(ms + EPS)
                x_lo = x_lo * r * gf_ref[:, :half]
                x_hi = x_hi * r * gf_ref[:, half:]
            out_ref[:, :half] = x_lo
            out_ref[:, half:] = x_hi


def _combine_call(dest_tiles, ys, wts_col, xp, mod, g_final, seq, final):
    t, d = xp.shape
    tm = TM_TOK
    nt = t // tm
    tiles_per_seq = seq // tm
    return pl.pallas_call(
        functools.partial(_combine_kernel, final=final),
        grid=(nt,),
        in_specs=[
            pl.BlockSpec((1, 1, TOP_K * tm), lambda i: (jnp.minimum(i + 1, nt - 1), 0, 0),
                         memory_space=pltpu.SMEM),
            pl.BlockSpec((1, 1, TOP_K * tm), lambda i: (0, 0, 0), memory_space=pltpu.SMEM),
            pl.BlockSpec(memory_space=pl.ANY),
            pl.BlockSpec((tm, TOP_K), lambda i: (i, 0)),
            pl.BlockSpec((tm, d), lambda i: (i, 0)),
            pl.BlockSpec((1, N_MOD, d), lambda i: (i // tiles_per_seq, 0, 0)),
            pl.BlockSpec((1, d), lambda i: (0, 0)),
        ],
        out_specs=pl.BlockSpec((tm, d), lambda i: (i, 0)),
        out_shape=jax.ShapeDtypeStruct((t, d), F32),
        scratch_shapes=[
            pltpu.VMEM((2, TOP_K, tm, ys.shape[1]), ys.dtype),
            pltpu.SemaphoreType.DMA((2,)),
        ],
        compiler_params=_cparams(("arbitrary",)),
        name="moe_combine",
    )(dest_tiles, dest_tiles, ys, wts_col, xp, mod, g_final)


def _reorder_w_in(w):
    cq, ckv, kpe, gb, gc, u, la, lc = jnp.split(
        w, [512, 768, 832, 1856, 2880, 3904, 5952], axis=1)
    pad = jnp.zeros((w.shape[0], N_PROJ - 8000), w.dtype)
    return jnp.concatenate([la, lc, gb, gc, u, cq, ckv, kpe, pad], axis=1).astype(BF16)


def kernel(x, c, positions, w_ada, b_ada, g_mix, w_in, g_q, w_uq, g_kv, w_ukv, w_o_attn, conv_w,
           w_conv_out, w_out, g_ffn, w_router, b_router, w1_e, w3_e, w2_e, w1_s, w3_s, w2_s, g_final):
    batch, seq, d = x.shape
    depth = w_ada.shape[0]
    t = batch * seq
    xf = x.reshape(t, d)

    c8 = jnp.zeros((8, d), F32).at[:batch].set(c)
    mod_all = _ada_call(c8, w_ada, b_ada.reshape(depth, 1, N_MOD * d))

    inv = 1.0 / (ROPE_THETA ** (jnp.arange(0, QK_ROPE, 2, dtype=F32) / QK_ROPE))
    inv128 = jnp.concatenate([inv, inv, jnp.zeros((LANES - QK_ROPE,), F32)]).reshape(1, LANES)
    cos_t, sin_t = _rope_call(positions.reshape(t, 1), inv128)

    for l in range(depth):
        mod = mod_all[l, :batch].reshape(batch, N_MOD, d)
        wq = jnp.pad(w_uq[l], ((0, 0), (0, 0), (0, HEAD_PAD - QK_HEAD))).reshape(
            Q_LORA, N_HEADS * HEAD_PAD).astype(BF16)
        wk = w_ukv[l][:, :, :QK_NOPE].reshape(KV_LORA, N_HEADS * QK_NOPE).astype(BF16)
        wv = w_ukv[l][:, :, QK_NOPE:].reshape(KV_LORA, N_HEADS * V_HEAD).astype(BF16)

        proj = _in_call(xf, g_mix[l].reshape(1, d), mod, _reorder_w_in(w_in[l]), seq)
        q, k, v = _qkv_call(proj, cos_t, sin_t, g_q[l].reshape(1, Q_LORA),
                            g_kv[l].reshape(1, KV_LORA), wq, wk, wv)
        o = _attn_call(q, k, v, batch, seq)
        xf = _merge_call(o, proj, xf, mod, conv_w[l].reshape(3, CONV_WIDTH),
                         w_o_attn[l].astype(BF16), w_conv_out[l].astype(BF16),
                         w_out[l].astype(BF16), seq)

        h, xp, idx_t, wts_t, rank_t, cnt = _route_call(
            xf, g_ffn[l].reshape(1, d), mod, w_router[l].T, b_router[l].reshape(N_EXPERTS, 1),
            w1_s[l].astype(BF16), w3_s[l].astype(BF16), w2_s[l].astype(BF16), seq)
        tile_info, nact, zero_start, dest_tiles, n_rows = _row_plan(idx_t, rank_t, cnt, t)
        xs = _dispatch_call(zero_start, dest_tiles, h, n_rows)
        ys = _expert_call(l, tile_info, nact, xs, w1_e, w3_e, w2_e)
        xf = _combine_call(dest_tiles, ys, wts_t.T, xp, mod, g_final.reshape(1, d), seq,
                           final=(l == depth - 1))

    return xf.reshape(batch, seq, d)
```

```python
import functools

import jax
import jax.numpy as jnp
from jax import lax
from jax.experimental import pallas as pl
from jax.experimental.pallas import tpu as pltpu

F32 = jnp.float32
BF16 = jnp.bfloat16

N_HEADS = 16
Q_LORA = 512
KV_LORA = 256
QK_NOPE = 128
QK_ROPE = 64
V_HEAD = 128
QK_HEAD = QK_NOPE + QK_ROPE
ROPE_THETA = 10000.0
CONV_WIDTH = 1024
N_EXPERTS = 64
TOP_K = 8
N_GROUPS = 8
TOPK_GROUPS = 4
EXPERTS_PER_GROUP = N_EXPERTS // N_GROUPS
ROUTED_SCALE = 2.5
N_MOD = 6
EPS = 1e-6
LOG2_E = 1.4426950408889634

LANES = 128
BF16_SUBLANES = 16
HEAD_PAD = 2 * LANES

COL_LA = 0
COL_LC = 2048
COL_GB = 4096
COL_GC = 5120
COL_U = 6144
COL_CQ = 7168
COL_CKV = 7680
COL_KPE = 7936
N_PROJ = 8064

VMEM_LIMIT = 56 * 1024 * 1024

TM_IN = 1024
TN_IN = 1152
TM_QKV = 512
TQ_ATTN = 1024
HEADS_PER_STEP = 1
TK_ATTN = 512
TM_MERGE = 256
TM_ROUTE = 512
TM_EXP = 256
TM_TOK = 128
TN_ADA = 2048
DMA_UNROLL = 8


def _cparams(sem):
    return pltpu.CompilerParams(dimension_semantics=sem, vmem_limit_bytes=VMEM_LIMIT)


def _rms(x):
    return x * lax.rsqrt(jnp.mean(x * x, axis=-1, keepdims=True) + EPS)


def _silu(x):
    return x * jax.nn.sigmoid(x)


HI_HALF = 0xFFFF0000


def _pack_pair(lo, hi):
    lo_bits = lax.bitcast_convert_type(lo.astype(BF16).astype(F32), jnp.uint32) >> 16
    hi_bits = lax.bitcast_convert_type(hi.astype(BF16).astype(F32), jnp.uint32) & jnp.uint32(HI_HALF)
    return hi_bits | lo_bits


def _unpack_pair(u):
    lo = lax.bitcast_convert_type(u << 16, F32)
    hi = lax.bitcast_convert_type(u & jnp.uint32(HI_HALF), F32)
    return lo, hi


def _ada_kernel(c_ref, w_ref, b_ref, o_ref):
    ca = _silu(c_ref[...]).astype(BF16)
    o_ref[0] = jnp.dot(ca, w_ref[0].astype(BF16), preferred_element_type=F32) + b_ref[0]


def _ada_call(c8, w_ada, b_ada3):
    depth, d, n = w_ada.shape
    return pl.pallas_call(
        _ada_kernel,
        grid=(depth, n // TN_ADA),
        in_specs=[
            pl.BlockSpec((8, d), lambda l, j: (0, 0)),
            pl.BlockSpec((1, d, TN_ADA), lambda l, j: (l, 0, j)),
            pl.BlockSpec((1, 1, TN_ADA), lambda l, j: (l, 0, j)),
        ],
        out_specs=pl.BlockSpec((1, 8, TN_ADA), lambda l, j: (l, 0, j)),
        out_shape=jax.ShapeDtypeStruct((depth, 8, n), F32),
        compiler_params=_cparams(("parallel", "parallel")),
        name="ada_mod",
    )(c8, w_ada, b_ada3)


def _rope_kernel(pos_ref, inv_ref, cos_ref, sin_ref):
    ang = pos_ref[...].astype(F32) * inv_ref[...]
    lane = lax.broadcasted_iota(jnp.int32, ang.shape, 1)
    half = QK_ROPE // 2
    cos_ref[...] = jnp.where(lane < QK_ROPE, jnp.cos(ang), 0.0)
    s = jnp.sin(ang)
    sin_ref[...] = jnp.where(lane < half, -s, jnp.where(lane < QK_ROPE, s, 0.0))


def _rope_call(pos_col, inv128):
    t = pos_col.shape[0]
    tm = 1024
    return pl.pallas_call(
        _rope_kernel,
        grid=(t // tm,),
        in_specs=[pl.BlockSpec((tm, 1), lambda i: (i, 0)),
                  pl.BlockSpec((1, LANES), lambda i: (0, 0))],
        out_specs=[pl.BlockSpec((tm, LANES), lambda i: (i, 0)),
                   pl.BlockSpec((tm, LANES), lambda i: (i, 0))],
        out_shape=[jax.ShapeDtypeStruct((t, LANES), F32)] * 2,
        compiler_params=_cparams(("parallel",)),
        name="rope_tables",
    )(pos_col, inv128)


def _in_kernel(x_ref, g_ref, mod_ref, w_ref, o_ref, h_ref):
    @pl.when(pl.program_id(1) == 0)
    def _():
        m = mod_ref[0]
        h = _rms(x_ref[...]) * g_ref[...]
        h_ref[...] = (h * (1.0 + m[1:2]) + m[0:1]).astype(BF16)

    o_ref[...] = jnp.dot(h_ref[...], w_ref[...], preferred_element_type=F32).astype(o_ref.dtype)


def _in_call(x, g, mod, w_in_r, seq):
    t, d = x.shape
    tiles_per_seq = seq // TM_IN
    return pl.pallas_call(
        _in_kernel,
        grid=(t // TM_IN, N_PROJ // TN_IN),
        in_specs=[
            pl.BlockSpec((TM_IN, d), lambda i, j: (i, 0)),
            pl.BlockSpec((1, d), lambda i, j: (0, 0)),
            pl.BlockSpec((1, N_MOD, d), lambda i, j: (i // tiles_per_seq, 0, 0)),
            pl.BlockSpec((d, TN_IN), lambda i, j: (0, j)),
        ],
        out_specs=pl.BlockSpec((TM_IN, TN_IN), lambda i, j: (i, j)),
        out_shape=jax.ShapeDtypeStruct((t, N_PROJ), BF16),
        scratch_shapes=[pltpu.VMEM((TM_IN, d), BF16)],
        compiler_params=_cparams(("parallel", "arbitrary")),
        name="in_proj",
    )(x, g, mod, w_in_r)


def _rope128(t, cos, sin_signed):
    lane = lax.broadcasted_iota(jnp.int32, t.shape, 1)
    half = QK_ROPE // 2
    partner = jnp.where(lane < half, pltpu.roll(t, LANES - half, 1), pltpu.roll(t, half, 1))
    return t * cos + partner * sin_signed


def _qkv_kernel(cq_ref, ckv_ref, kpe_ref, cos_ref, sin_ref, gq_ref, gkv_ref,
                wq_ref, wkv_ref, q_ref, k_ref, v_ref):
    cos = cos_ref[...]
    sin = sin_ref[...]
    qn = (_rms(cq_ref[...].astype(F32)) * gq_ref[...]).astype(BF16)
    q = jnp.dot(qn, wq_ref[...], preferred_element_type=F32) * (QK_HEAD ** -0.5 * LOG2_E)
    cn = (_rms(ckv_ref[...].astype(F32)) * gkv_ref[...]).astype(BF16)
    kv = jnp.dot(cn, wkv_ref[...], preferred_element_type=F32).astype(BF16)
    kr = _rope128(kpe_ref[...].astype(F32), cos, sin).astype(BF16)
    lane = lax.broadcasted_iota(jnp.int32, kr.shape, 1)
    ones_col = jnp.where(lane == 0, 1.0, 0.0).astype(BF16)
    for h in range(N_HEADS):
        base = h * HEAD_PAD
        v_ref[:, base:base + V_HEAD] = kv[:, base + QK_NOPE:base + HEAD_PAD]
        v_ref[:, base + V_HEAD:base + HEAD_PAD] = ones_col
        q_ref[:, base:base + LANES] = q[:, base:base + LANES].astype(BF16)
        q_ref[:, base + LANES:base + HEAD_PAD] = _rope128(
            q[:, base + LANES:base + HEAD_PAD], cos, sin).astype(BF16)
        k_ref[:, base:base + LANES] = kv[:, base:base + QK_NOPE]
        k_ref[:, base + LANES:base + HEAD_PAD] = kr


def _qkv_call(proj, cos, sin, gq, gkv, wq, wkv):
    t = proj.shape[0]
    tm = TM_QKV
    return pl.pallas_call(
        _qkv_kernel,
        grid=(t // tm,),
        in_specs=[
            pl.BlockSpec((tm, Q_LORA), lambda i: (i, COL_CQ // Q_LORA)),
            pl.BlockSpec((tm, KV_LORA), lambda i: (i, COL_CKV // KV_LORA)),
            pl.BlockSpec((tm, LANES), lambda i: (i, COL_KPE // LANES)),
            pl.BlockSpec((tm, LANES), lambda i: (i, 0)),
            pl.BlockSpec((tm, LANES), lambda i: (i, 0)),
            pl.BlockSpec((1, Q_LORA), lambda i: (0, 0)),
            pl.BlockSpec((1, KV_LORA), lambda i: (0, 0)),
            pl.BlockSpec(wq.shape, lambda i: (0, 0)),
            pl.BlockSpec(wkv.shape, lambda i: (0, 0)),
        ],
        out_specs=[
            pl.BlockSpec((tm, N_HEADS * HEAD_PAD), lambda i: (i, 0)),
            pl.BlockSpec((tm, N_HEADS * HEAD_PAD), lambda i: (i, 0)),
            pl.BlockSpec((tm, N_HEADS * HEAD_PAD), lambda i: (i, 0)),
        ],
        out_shape=[
            jax.ShapeDtypeStruct((t, N_HEADS * HEAD_PAD), BF16),
            jax.ShapeDtypeStruct((t, N_HEADS * HEAD_PAD), BF16),
            jax.ShapeDtypeStruct((t, N_HEADS * HEAD_PAD), BF16),
        ],
        compiler_params=_cparams(("parallel",)),
        name="qkv_build",
    )(proj, proj, proj, cos, sin, gq, gkv, wq, wkv)


def _attn_kernel(q_ref, k_ref, v_ref, o_ref):
    tq = q_ref.shape[0]
    n_chunks = k_ref.shape[0] // TK_ATTN
    for hh in range(HEADS_PER_STEP):
        qk = slice(hh * HEAD_PAD, (hh + 1) * HEAD_PAD)
        vo = slice(hh * V_HEAD, (hh + 1) * V_HEAD)
        q = q_ref[:, qk]
        m = jnp.full((tq, 1), -jnp.inf, F32)
        acc = jnp.zeros((tq, HEAD_PAD), F32)
        for j in range(n_chunks):
            rows = slice(j * TK_ATTN, (j + 1) * TK_ATTN)
            s = lax.dot_general(q, k_ref[rows, qk], (((1,), (1,)), ((), ())),
                                preferred_element_type=F32)
            m_new = jnp.maximum(m, jnp.max(s, axis=-1, keepdims=True))
            alpha = jnp.exp2(m - m_new)
            p = jnp.exp2(s - m_new)
            acc = alpha * acc + jnp.dot(p.astype(BF16), v_ref[rows, qk],
                                        preferred_element_type=F32)
            m = m_new
        o_ref[:, vo] = (acc[:, :V_HEAD] / acc[:, V_HEAD:V_HEAD + 1]).astype(o_ref.dtype)


def _attn_call(q, k, v, batch, seq):
    t = q.shape[0]
    nq = seq // TQ_ATTN
    hs = HEADS_PER_STEP
    return pl.pallas_call(
        _attn_kernel,
        grid=(batch, N_HEADS // hs, nq),
        in_specs=[
            pl.BlockSpec((TQ_ATTN, hs * HEAD_PAD), lambda b, h, i: (b * nq + i, h)),
            pl.BlockSpec((seq, hs * HEAD_PAD), lambda b, h, i: (b, h)),
            pl.BlockSpec((seq, hs * HEAD_PAD), lambda b, h, i: (b, h)),
        ],
        out_specs=pl.BlockSpec((TQ_ATTN, hs * V_HEAD), lambda b, h, i: (b * nq + i, h)),
        out_shape=jax.ShapeDtypeStruct((t, N_HEADS * V_HEAD), BF16),
        compiler_params=_cparams(("parallel", "parallel", "arbitrary")),
        name="mla_attention",
    )(q, k, v)


def _merge_kernel(o_ref, la_ref, lc_ref, gb_ref, gc_ref, u_ref, gcp_ref, up_ref, gcn_ref, un_ref,
                  x_ref, mod_ref, cw_ref, wo_ref, wco_ref, wout_ref, out_ref, *, tiles_per_seq):
    i = pl.program_id(0)
    tm = x_ref.shape[0]
    pos = i % tiles_per_seq
    has_prev = (pos != 0).astype(F32)
    has_next = (pos != tiles_per_seq - 1).astype(F32)
    z = gc_ref[...].astype(F32) * u_ref[...].astype(F32)
    last = BF16_SUBLANES - 1
    z_before = gcp_ref[last:last + 1, :].astype(F32) * up_ref[last:last + 1, :].astype(F32) * has_prev
    z_after = gcn_ref[0:1, :].astype(F32) * un_ref[0:1, :].astype(F32) * has_next
    row = lax.broadcasted_iota(jnp.int32, z.shape, 0)
    z_prev = jnp.where(row == 0, z_before, pltpu.roll(z, 1, 0))
    z_next = jnp.where(row == tm - 1, z_after, pltpu.roll(z, tm - 1, 0))
    cw = cw_ref[...]
    conv = cw[0:1] * z_prev + cw[1:2] * z + cw[2:3] * z_next
    y_conv = jnp.dot((gb_ref[...].astype(F32) * conv).astype(BF16), wco_ref[...],
                     preferred_element_type=F32)
    y_attn = jnp.dot(o_ref[...], wo_ref[...], preferred_element_type=F32)
    mix = (jax.nn.sigmoid(la_ref[...].astype(F32)) * y_attn
           + jax.nn.sigmoid(lc_ref[...].astype(F32)) * y_conv)
    y = jnp.dot(mix.astype(BF16), wout_ref[...], preferred_element_type=F32)
    out_ref[...] = x_ref[...] + mod_ref[0][2:3] * y


def _merge_call(o, proj, x, mod, conv_w, w_o, w_co, w_out, seq):
    t, d = x.shape
    tm = TM_MERGE
    tiles_per_seq = seq // tm
    hb = tm // BF16_SUBLANES
    n_halo = t // BF16_SUBLANES
    cw_blocks = CONV_WIDTH
    const = dict(pipeline_mode=pl.Buffered(1))
    prev_map = lambda c: (lambda i: (jnp.maximum(i * hb - 1, 0), c))
    next_map = lambda c: (lambda i: (jnp.minimum((i + 1) * hb, n_halo - 1), c))
    return pl.pallas_call(
        functools.partial(_merge_kernel, tiles_per_seq=tiles_per_seq),
        grid=(t // tm,),
        in_specs=[
            pl.BlockSpec((tm, d), lambda i: (i, 0)),
            pl.BlockSpec((tm, d), lambda i: (i, COL_LA // d)),
            pl.BlockSpec((tm, d), lambda i: (i, COL_LC // d)),
            pl.BlockSpec((tm, cw_blocks), lambda i: (i, COL_GB // cw_blocks)),
            pl.BlockSpec((tm, cw_blocks), lambda i: (i, COL_GC // cw_blocks)),
            pl.BlockSpec((tm, cw_blocks), lambda i: (i, COL_U // cw_blocks)),
            pl.BlockSpec((BF16_SUBLANES, cw_blocks), prev_map(COL_GC // cw_blocks)),
            pl.BlockSpec((BF16_SUBLANES, cw_blocks), prev_map(COL_U // cw_blocks)),
            pl.BlockSpec((BF16_SUBLANES, cw_blocks), next_map(COL_GC // cw_blocks)),
            pl.BlockSpec((BF16_SUBLANES, cw_blocks), next_map(COL_U // cw_blocks)),
            pl.BlockSpec((tm, d), lambda i: (i, 0)),
            pl.BlockSpec((1, N_MOD, d), lambda i: (i // tiles_per_seq, 0, 0)),
            pl.BlockSpec(conv_w.shape, lambda i: (0, 0)),
            pl.BlockSpec(w_o.shape, lambda i: (0, 0), **const),
            pl.BlockSpec(w_co.shape, lambda i: (0, 0), **const),
            pl.BlockSpec(w_out.shape, lambda i: (0, 0), **const),
        ],
        out_specs=pl.BlockSpec((tm, d), lambda i: (i, 0)),
        out_shape=jax.ShapeDtypeStruct((t, d), F32),
        compiler_params=_cparams(("parallel",)),
        name="conv_merge_out",
    )(o, proj, proj, proj, proj, proj, proj, proj, proj, proj, x, mod, conv_w, w_o, w_co, w_out)


def _first_max(vals, iota, sentinel):
    m = jnp.max(vals, axis=0, keepdims=True)
    first = jnp.min(jnp.where(vals == m, iota, sentinel), axis=0, keepdims=True)
    return m, first, iota == first


def _route_kernel(x_ref, g_ref, mod_ref, wr_ref, br_ref, w1_ref, w3_ref, w2_ref, tri_ref,
                  h_ref, xp_ref, idx_ref, wts_ref, rank_ref, cnt_ref, run_ref):
    @pl.when(pl.program_id(0) == 0)
    def _():
        run_ref[...] = jnp.zeros_like(run_ref)

    m = mod_ref[0]
    x = x_ref[...]
    h = (_rms(x) * g_ref[...]) * (1.0 + m[4:5]) + m[3:4]
    half = h.shape[1] // 2
    h_ref[...] = _pack_pair(h[:, :half], h[:, half:])
    hb = h.astype(BF16)
    a = jnp.dot(hb, w1_ref[...], preferred_element_type=F32)
    b = jnp.dot(hb, w3_ref[...], preferred_element_type=F32)
    shared = jnp.dot((_silu(a) * b).astype(BF16), w2_ref[...], preferred_element_type=F32)
    xp_ref[...] = x + m[5:6] * shared

    logits = lax.dot_general(wr_ref[...], h, (((1,), (1,)), ((), ())),
                             precision=lax.Precision.HIGHEST, preferred_element_type=F32)
    scores = jax.nn.sigmoid(logits)
    biased = scores + br_ref[...]
    tm = scores.shape[1]
    neg = -jnp.inf
    g = EXPERTS_PER_GROUP
    iota_g = lax.broadcasted_iota(jnp.int32, (g, tm), 0).astype(F32)
    group_rows = []
    for gi in range(N_GROUPS):
        blk = biased[gi * g:(gi + 1) * g, :]
        m1, _, hit = _first_max(blk, iota_g, g)
        m2 = jnp.max(jnp.where(hit, neg, blk), axis=0, keepdims=True)
        group_rows.append(m1 + m2)
    gs = jnp.concatenate(group_rows, axis=0)
    iota_n = lax.broadcasted_iota(jnp.int32, (N_GROUPS, tm), 0).astype(F32)
    chosen = jnp.zeros((N_GROUPS, tm), F32)
    for _ in range(TOPK_GROUPS):
        _, _, hit = _first_max(gs, iota_n, N_GROUPS)
        chosen = jnp.where(hit, 1.0, chosen)
        gs = jnp.where(hit, neg, gs)
    allowed = jnp.concatenate(
        [jnp.broadcast_to(chosen[gi:gi + 1, :], (g, tm)) for gi in range(N_GROUPS)], axis=0)
    cand = jnp.where(allowed > 0.0, biased, neg)
    iota_e = lax.broadcasted_iota(jnp.int32, (N_EXPERTS, tm), 0).astype(F32)

    run = run_ref[:, 0:1]
    tri = tri_ref[...]
    idx_rows, w_rows, rank_rows = [], [], []
    for _ in range(TOP_K):
        _, first, hit = _first_max(cand, iota_e, N_EXPERTS)
        idx_rows.append(first)
        w_rows.append(jnp.sum(jnp.where(hit, scores, 0.0), axis=0, keepdims=True))
        cand = jnp.where(hit, neg, cand)
        prefix = jnp.dot(jnp.where(hit, 1.0, 0.0).astype(BF16), tri, preferred_element_type=F32)
        rank_rows.append(jnp.sum(jnp.where(hit, run + prefix - 1.0, 0.0), axis=0, keepdims=True))
        run = run + prefix[:, tm - 1:tm]
    w = jnp.concatenate(w_rows, axis=0)
    idx_ref[...] = jnp.concatenate(idx_rows, axis=0).astype(jnp.int32)
    wts_ref[...] = w / jnp.sum(w, axis=0, keepdims=True) * ROUTED_SCALE
    rank_ref[...] = jnp.concatenate(rank_rows, axis=0).astype(jnp.int32)
    run_full = jnp.broadcast_to(run, run_ref.shape)
    run_ref[...] = run_full
    cnt_ref[...] = run_full


def _route_call(x, g, mod, wr_t, br_col, w1s, w3s, w2s, seq):
    t, d = x.shape
    tm = TM_ROUTE
    tiles_per_seq = seq // tm
    tri = (jnp.arange(tm)[:, None] <= jnp.arange(tm)[None, :]).astype(BF16)
    return pl.pallas_call(
        _route_kernel,
        grid=(t // tm,),
        in_specs=[
            pl.BlockSpec((tm, d), lambda i: (i, 0)),
            pl.BlockSpec((1, d), lambda i: (0, 0)),
            pl.BlockSpec((1, N_MOD, d), lambda i: (i // tiles_per_seq, 0, 0)),
            pl.BlockSpec(wr_t.shape, lambda i: (0, 0)),
            pl.BlockSpec(br_col.shape, lambda i: (0, 0)),
            pl.BlockSpec(w1s.shape, lambda i: (0, 0)),
            pl.BlockSpec(w3s.shape, lambda i: (0, 0)),
            pl.BlockSpec(w2s.shape, lambda i: (0, 0)),
            pl.BlockSpec((tm, tm), lambda i: (0, 0)),
        ],
        out_specs=[
            pl.BlockSpec((tm, d // 2), lambda i: (i, 0)),
            pl.BlockSpec((tm, d), lambda i: (i, 0)),
            pl.BlockSpec((TOP_K, tm), lambda i: (0, i)),
            pl.BlockSpec((TOP_K, tm), lambda i: (0, i)),
            pl.BlockSpec((TOP_K, tm), lambda i: (0, i)),
            pl.BlockSpec((N_EXPERTS, LANES), lambda i: (0, 0)),
        ],
        out_shape=[
            jax.ShapeDtypeStruct((t, d // 2), jnp.uint32),
            jax.ShapeDtypeStruct((t, d), F32),
            jax.ShapeDtypeStruct((TOP_K, t), jnp.int32),
            jax.ShapeDtypeStruct((TOP_K, t), F32),
            jax.ShapeDtypeStruct((TOP_K, t), jnp.int32),
            jax.ShapeDtypeStruct((N_EXPERTS, LANES), F32),
        ],
        scratch_shapes=[pltpu.VMEM((N_EXPERTS, LANES), F32)],
        compiler_params=_cparams(("arbitrary",)),
        name="moe_route",
    )(x, g, mod, wr_t, br_col, w1s, w3s, w2s, tri)


def _row_plan(idx_t, rank_t, cnt, n_tok):
    tm = TM_EXP
    n_tiles = TOP_K * n_tok // tm + N_EXPERTS
    counts = cnt[:, 0].astype(jnp.int32)
    padded = (counts + tm - 1) // tm * tm
    pend = jnp.cumsum(padded)
    pstart = pend - padded
    tile_start = jnp.arange(n_tiles, dtype=jnp.int32) * tm
    tile_e = jnp.minimum(jnp.sum((pend[None, :] <= tile_start[:, None]).astype(jnp.int32), axis=1),
                         N_EXPERTS - 1).astype(jnp.int32)
    nact = (pend[-1] // tm).astype(jnp.int32).reshape(1)
    eids = jnp.arange(N_EXPERTS, dtype=jnp.int32)
    nonempty = counts > 0
    after = eids[None, :] > eids[:, None]
    ordinal = jnp.sum((~after & nonempty[None, :]).astype(jnp.int32), axis=1) - 1
    next_e = jnp.min(jnp.where(after & nonempty[None, :], eids[None, :], N_EXPERTS), axis=1)
    is_first = (tile_start == pstart[tile_e]).astype(jnp.int32)
    tile_info = jnp.stack([tile_e, is_first, ordinal[tile_e] % 2, next_e[tile_e]], axis=0)
    zero_start = jnp.maximum(pend - tm, 0).astype(jnp.int32)
    experts = jnp.arange(N_EXPERTS, dtype=jnp.int32)[:, None, None]
    dest = rank_t + jnp.sum(jnp.where(idx_t[None] == experts, pstart[:, None, None], 0), axis=0)
    nt = n_tok // TM_TOK
    dest_tiles = dest.reshape(TOP_K, nt, TM_TOK).transpose(1, 0, 2).reshape(nt, 1, TOP_K * TM_TOK)
    return tile_info.astype(jnp.int32), nact, zero_start, dest_tiles.astype(jnp.int32), n_tiles * tm


def _dispatch_kernel(zs_ref, dest_ref, h_ref, xs_hbm, zbuf, zsem, sem):
    i = pl.program_id(0)
    tm = TM_TOK
    tme = zbuf.shape[0]

    def row_copy(j, dst_row):
        return pltpu.make_async_copy(h_ref.at[pl.ds(j, 1)], xs_hbm.at[pl.ds(dst_row, 1)], sem)

    @pl.when(i == 0)
    def _():
        zbuf[...] = jnp.zeros_like(zbuf)

        def zero_copy(e):
            start = pl.multiple_of(zs_ref[e], tme)
            return pltpu.make_async_copy(zbuf, xs_hbm.at[pl.ds(start, tme)], zsem)
        for e in range(N_EXPERTS):
            zero_copy(e).start()
        for e in range(N_EXPERTS):
            zero_copy(e).wait()

    for j in range(tm):
        for k in range(TOP_K):
            row_copy(j, dest_ref[0, 0, k * tm + j]).start(priority=k % 2)

    for j in range(tm):
        for _ in range(TOP_K):
            row_copy(j, 0).wait()


def _dispatch_call(zero_start, dest_tiles, hp, n_rows):
    t, dp = hp.shape
    nt = t // TM_TOK
    grid_spec = pltpu.PrefetchScalarGridSpec(
        num_scalar_prefetch=1,
        grid=(nt,),
        in_specs=[
            pl.BlockSpec((1, 1, TOP_K * TM_TOK), lambda i, zs: (i, 0, 0), memory_space=pltpu.SMEM),
            pl.BlockSpec((TM_TOK, dp), lambda i, zs: (i, 0)),
        ],
        out_specs=pl.BlockSpec(memory_space=pl.ANY),
        scratch_shapes=[
            pltpu.VMEM((TM_EXP, dp), hp.dtype),
            pltpu.SemaphoreType.DMA(()),
            pltpu.SemaphoreType.DMA(()),
        ],
    )
    return pl.pallas_call(
        _dispatch_kernel,
        grid_spec=grid_spec,
        out_shape=jax.ShapeDtypeStruct((n_rows, dp), hp.dtype),
        compiler_params=_cparams(("arbitrary",)),
        name="moe_dispatch",
    )(zero_start, dest_tiles, hp)


def _expert_kernel(ti_ref, nact_ref, x_ref, w1_hbm, w3_hbm, w2_hbm, y_ref,
                   w1f, w3f, w2f, w1b, w3b, w2b, wsem, *, layer):
    i = pl.program_id(0)

    def weight_copies(e, s):
        return (pltpu.make_async_copy(w1_hbm.at[layer, e], w1f.at[s], wsem.at[s]),
                pltpu.make_async_copy(w3_hbm.at[layer, e], w3f.at[s], wsem.at[s]),
                pltpu.make_async_copy(w2_hbm.at[layer, e], w2f.at[s], wsem.at[s]))

    @pl.when(i < nact_ref[0])
    def _():
        @pl.when(ti_ref[1, i] == 1)
        def _():
            e = ti_ref[0, i]
            nxt = ti_ref[3, i]
            for s in range(2):
                @pl.when(ti_ref[2, i] == s)
                def _():
                    @pl.when(i == 0)
                    def _():
                        for c in weight_copies(e, s):
                            c.start()
                    for c in weight_copies(e, s):
                        c.wait()

                    @pl.when(nxt < N_EXPERTS)
                    def _():
                        for c in weight_copies(nxt, 1 - s):
                            c.start()
                    w1b[...] = w1f[s].astype(BF16)
                    w3b[...] = w3f[s].astype(BF16)
                    w2b[...] = w2f[s].astype(BF16)

        half = x_ref.shape[1]
        x_lo, x_hi = _unpack_pair(x_ref[...])
        x_lo = x_lo.astype(BF16)
        x_hi = x_hi.astype(BF16)
        a = (jnp.dot(x_lo, w1b[:half, :], preferred_element_type=F32)
             + jnp.dot(x_hi, w1b[half:, :], preferred_element_type=F32))
        b = (jnp.dot(x_lo, w3b[:half, :], preferred_element_type=F32)
             + jnp.dot(x_hi, w3b[half:, :], preferred_element_type=F32))
        g = (_silu(a) * b).astype(BF16)
        y_ref[...] = _pack_pair(jnp.dot(g, w2b[:, :half], preferred_element_type=F32),
                                jnp.dot(g, w2b[:, half:], preferred_element_type=F32))


def _expert_call(layer, tile_info, nact, xs, w1_e, w3_e, w2_e):
    tm = TM_EXP
    n_rows, dp = xs.shape
    d = 2 * dp
    n_tiles = n_rows // tm
    de = w1_e.shape[-1]
    row_map = lambda i, ti, na: (jnp.minimum(i, na[0] - 1), 0)
    grid_spec = pltpu.PrefetchScalarGridSpec(
        num_scalar_prefetch=2,
        grid=(n_tiles,),
        in_specs=[
            pl.BlockSpec((tm, dp), row_map),
            pl.BlockSpec(memory_space=pl.ANY),
            pl.BlockSpec(memory_space=pl.ANY),
            pl.BlockSpec(memory_space=pl.ANY),
        ],
        out_specs=pl.BlockSpec((tm, dp), row_map),
        scratch_shapes=[
            pltpu.VMEM((2, d, de), F32),
            pltpu.VMEM((2, d, de), F32),
            pltpu.VMEM((2, de, d), F32),
            pltpu.VMEM((d, de), BF16),
            pltpu.VMEM((d, de), BF16),
            pltpu.VMEM((de, d), BF16),
            pltpu.SemaphoreType.DMA((2,)),
        ],
    )
    return pl.pallas_call(
        functools.partial(_expert_kernel, layer=layer),
        grid_spec=grid_spec,
        out_shape=jax.ShapeDtypeStruct((n_rows, dp), xs.dtype),
        compiler_params=_cparams(("arbitrary",)),
        name="moe_experts",
    )(tile_info, nact, xs, w1_e, w3_e, w2_e)


def _combine_kernel(dnext_ref, dfirst_ref, y_hbm, w_ref, xp_ref, mod_ref, gf_ref, out_ref,
                    ybuf, sem, *, final):
    i = pl.program_id(0)
    n = pl.num_programs(0)
    tm = TM_TOK
    slot = i % 2

    def row_copy(src_row, k, j, s):
        return pltpu.make_async_copy(y_hbm.at[pl.ds(src_row, 1)], ybuf.at[s, k, pl.ds(j, 1)],
                                     sem.at[s])

    def start_rows(d_ref, s):
        def body(j, carry):
            for k in range(TOP_K):
                row_copy(d_ref[0, 0, k * tm + j], k, j, s).start(priority=k % 2)
            return carry
        lax.fori_loop(0, tm, body, 0, unroll=DMA_UNROLL)

    def wait_rows(s):
        def body(j, carry):
            for k in range(TOP_K):
                row_copy(0, k, j, s).wait()
            return carry
        lax.fori_loop(0, tm, body, 0, unroll=DMA_UNROLL)

    @pl.when(i == 0)
    def _():
        start_rows(dfirst_ref, 0)

    for s in range(2):
        @pl.when((i + 1 < n) & (slot == 1 - s))
        def _():
            start_rows(dnext_ref, s)

    for s in range(2):
        @pl.when(slot == s)
        def _():
            wait_rows(s)
            w = w_ref[...]
            half = ybuf.shape[3]
            acc_lo, acc_hi = _unpack_pair(ybuf[s, 0])
            acc_lo = acc_lo * w[:, 0:1]
            acc_hi = acc_hi * w[:, 0:1]
            for k in range(1, TOP_K):
                y_lo, y_hi = _unpack_pair(ybuf[s, k])
                acc_lo = acc_lo + y_lo * w[:, k:k + 1]
                acc_hi = acc_hi + y_hi * w[:, k:k + 1]
            gate = mod_ref[0][5:6]
            x_lo = xp_ref[:, :half] + gate[:, :half] * acc_lo
            x_hi = xp_ref[:, half:] + gate[:, half:] * acc_hi
            if final:
                ms = (jnp.sum(x_lo * x_lo, axis=-1, keepdims=True)
                      + jnp.sum(x_hi * x_hi, axis=-1, keepdims=True)) / (2 * half)
                r = lax.rsqrt(ms + EPS)
                x_lo = x_lo * r * gf_ref[:, :half]
                x_hi = x_hi * r * gf_ref[:, half:]
            out_ref[:, :half] = x_lo
            out_ref[:, half:] = x_hi


def _combine_call(dest_tiles, ys, wts_col, xp, mod, g_final, seq, final):
    t, d = xp.shape
    tm = TM_TOK
    nt = t // tm
    tiles_per_seq = seq // tm
    return pl.pallas_call(
        functools.partial(_combine_kernel, final=final),
        grid=(nt,),
        in_specs=[
            pl.BlockSpec((1, 1, TOP_K * tm), lambda i: (jnp.minimum(i + 1, nt - 1), 0, 0),
                         memory_space=pltpu.SMEM),
            pl.BlockSpec((1, 1, TOP_K * tm), lambda i: (0, 0, 0), memory_space=pltpu.SMEM),
            pl.BlockSpec(memory_space=pl.ANY),
            pl.BlockSpec((tm, TOP_K), lambda i: (i, 0)),
            pl.BlockSpec((tm, d), lambda i: (i, 0)),
            pl.BlockSpec((1, N_MOD, d), lambda i: (i // tiles_per_seq, 0, 0)),
            pl.BlockSpec((1, d), lambda i: (0, 0)),
        ],
        out_specs=pl.BlockSpec((tm, d), lambda i: (i, 0)),
        out_shape=jax.ShapeDtypeStruct((t, d), F32),
        scratch_shapes=[
            pltpu.VMEM((2, TOP_K, tm, ys.shape[1]), ys.dtype),
            pltpu.SemaphoreType.DMA((2,)),
        ],
        compiler_params=_cparams(("arbitrary",)),
        name="moe_combine",
    )(dest_tiles, dest_tiles, ys, wts_col, xp, mod, g_final)


def _reorder_w_in(w):
    cq, ckv, kpe, gb, gc, u, la, lc = jnp.split(
        w, [512, 768, 832, 1856, 2880, 3904, 5952], axis=1)
    pad = jnp.zeros((w.shape[0], N_PROJ - 8000), w.dtype)
    return jnp.concatenate([la, lc, gb, gc, u, cq, ckv, kpe, pad], axis=1).astype(BF16)


def kernel(x, c, positions, w_ada, b_ada, g_mix, w_in, g_q, w_uq, g_kv, w_ukv, w_o_attn, conv_w,
           w_conv_out, w_out, g_ffn, w_router, b_router, w1_e, w3_e, w2_e, w1_s, w3_s, w2_s, g_final):
    batch, seq, d = x.shape
    depth = w_ada.shape[0]
    t = batch * seq
    xf = x.reshape(t, d)

    c8 = jnp.zeros((8, d), F32).at[:batch].set(c)
    mod_all = _ada_call(c8, w_ada, b_ada.reshape(depth, 1, N_MOD * d))

    inv = 1.0 / (ROPE_THETA ** (jnp.arange(0, QK_ROPE, 2, dtype=F32) / QK_ROPE))
    inv128 = jnp.concatenate([inv, inv, jnp.zeros((LANES - QK_ROPE,), F32)]).reshape(1, LANES)
    cos_t, sin_t = _rope_call(positions.reshape(t, 1), inv128)

    for l in range(depth):
        mod = mod_all[l, :batch].reshape(batch, N_MOD, d)
        wq = jnp.pad(w_uq[l], ((0, 0), (0, 0), (0, HEAD_PAD - QK_HEAD))).reshape(
            Q_LORA, N_HEADS * HEAD_PAD).astype(BF16)
        wkv = w_ukv[l].reshape(KV_LORA, N_HEADS * (QK_NOPE + V_HEAD)).astype(BF16)

        proj = _in_call(xf, g_mix[l].reshape(1, d), mod, _reorder_w_in(w_in[l]), seq)
        q, k, v = _qkv_call(proj, cos_t, sin_t, g_q[l].reshape(1, Q_LORA),
                            g_kv[l].reshape(1, KV_LORA), wq, wkv)
        o = _attn_call(q, k, v, batch, seq)
        xf = _merge_call(o, proj, xf, mod, conv_w[l].reshape(3, CONV_WIDTH),
                         w_o_attn[l].astype(BF16), w_conv_out[l].astype(BF16),
                         w_out[l].astype(BF16), seq)

        h, xp, idx_t, wts_t, rank_t, cnt = _route_call(
            xf, g_ffn[l].reshape(1, d), mod, w_router[l].T, b_router[l].reshape(N_EXPERTS, 1),
            w1_s[l].astype(BF16), w3_s[l].astype(BF16), w2_s[l].astype(BF16), seq)
        tile_info, nact, zero_start, dest_tiles, n_rows = _row_plan(idx_t, rank_t, cnt, t)
        xs = _dispatch_call(zero_start, dest_tiles, h, n_rows)
        ys = _expert_call(l, tile_info, nact, xs, w1_e, w3_e, w2_e)
        xf = _combine_call(dest_tiles, ys, wts_t.T, xp, mod, g_final.reshape(1, d), seq,
                           final=(l == depth - 1))

    return xf.reshape(batch, seq, d)
```

```python
import functools

import jax
import jax.numpy as jnp
from jax import lax
from jax.experimental import pallas as pl
from jax.experimental.pallas import tpu as pltpu

F32 = jnp.float32
BF16 = jnp.bfloat16

N_HEADS = 16
Q_LORA = 512
KV_LORA = 256
QK_NOPE = 128
QK_ROPE = 64
V_HEAD = 128
QK_HEAD = QK_NOPE + QK_ROPE
ROPE_THETA = 10000.0
CONV_WIDTH = 1024
N_EXPERTS = 64
TOP_K = 8
N_GROUPS = 8
TOPK_GROUPS = 4
EXPERTS_PER_GROUP = N_EXPERTS // N_GROUPS
ROUTED_SCALE = 2.5
N_MOD = 6
EPS = 1e-6
LOG2_E = 1.4426950408889634

LANES = 128
BF16_SUBLANES = 16
HEAD_PAD = 2 * LANES

COL_LA = 0
COL_LC = 2048
COL_GB = 4096
COL_GC = 5120
COL_U = 6144
COL_CQ = 7168
COL_CKV = 7680
COL_KPE = 7936
N_PROJ = 8064

VMEM_LIMIT = 56 * 1024 * 1024

TM_IN = 1024
TN_IN = 1152
TM_QKV = 512
TQ_ATTN = 1024
HEADS_PER_STEP = 1
TK_ATTN = 512
TM_MERGE = 256
TM_ROUTE = 512
TM_EXP = 256
TM_TOK = 128
TN_ADA = 2048
DMA_UNROLL = 8


def _cparams(sem):
    return pltpu.CompilerParams(dimension_semantics=sem, vmem_limit_bytes=VMEM_LIMIT)


def _rms(x):
    return x * lax.rsqrt(jnp.mean(x * x, axis=-1, keepdims=True) + EPS)


def _silu(x):
    return x * jax.nn.sigmoid(x)


HI_HALF = 0xFFFF0000


def _pack_pair(lo, hi):
    lo_bits = lax.bitcast_convert_type(lo.astype(BF16).astype(F32), jnp.uint32) >> 16
    hi_bits = lax.bitcast_convert_type(hi.astype(BF16).astype(F32), jnp.uint32) & jnp.uint32(HI_HALF)
    return hi_bits | lo_bits


def _unpack_pair(u):
    lo = lax.bitcast_convert_type(u << 16, F32)
    hi = lax.bitcast_convert_type(u & jnp.uint32(HI_HALF), F32)
    return lo, hi


def _ada_kernel(c_ref, w_ref, b_ref, o_ref):
    ca = _silu(c_ref[...]).astype(BF16)
    o_ref[0] = jnp.dot(ca, w_ref[0].astype(BF16), preferred_element_type=F32) + b_ref[0]


def _ada_call(c8, w_ada, b_ada3):
    depth, d, n = w_ada.shape
    return pl.pallas_call(
        _ada_kernel,
        grid=(depth, n // TN_ADA),
        in_specs=[
            pl.BlockSpec((8, d), lambda l, j: (0, 0)),
            pl.BlockSpec((1, d, TN_ADA), lambda l, j: (l, 0, j)),
            pl.BlockSpec((1, 1, TN_ADA), lambda l, j: (l, 0, j)),
        ],
        out_specs=pl.BlockSpec((1, 8, TN_ADA), lambda l, j: (l, 0, j)),
        out_shape=jax.ShapeDtypeStruct((depth, 8, n), F32),
        compiler_params=_cparams(("parallel", "parallel")),
        name="ada_mod",
    )(c8, w_ada, b_ada3)


def _rope_kernel(pos_ref, inv_ref, cos_ref, sin_ref):
    ang = pos_ref[...].astype(F32) * inv_ref[...]
    lane = lax.broadcasted_iota(jnp.int32, ang.shape, 1)
    half = QK_ROPE // 2
    cos_ref[...] = jnp.where(lane < QK_ROPE, jnp.cos(ang), 0.0)
    s = jnp.sin(ang)
    sin_ref[...] = jnp.where(lane < half, -s, jnp.where(lane < QK_ROPE, s, 0.0))


def _rope_call(pos_col, inv128):
    t = pos_col.shape[0]
    tm = 1024
    return pl.pallas_call(
        _rope_kernel,
        grid=(t // tm,),
        in_specs=[pl.BlockSpec((tm, 1), lambda i: (i, 0)),
                  pl.BlockSpec((1, LANES), lambda i: (0, 0))],
        out_specs=[pl.BlockSpec((tm, LANES), lambda i: (i, 0)),
                   pl.BlockSpec((tm, LANES), lambda i: (i, 0))],
        out_shape=[jax.ShapeDtypeStruct((t, LANES), F32)] * 2,
        compiler_params=_cparams(("parallel",)),
        name="rope_tables",
    )(pos_col, inv128)


def _in_kernel(x_ref, g_ref, mod_ref, w_ref, o_ref, h_ref):
    @pl.when(pl.program_id(1) == 0)
    def _():
        m = mod_ref[0]
        h = _rms(x_ref[...]) * g_ref[...]
        h_ref[...] = (h * (1.0 + m[1:2]) + m[0:1]).astype(BF16)

    o_ref[...] = jnp.dot(h_ref[...], w_ref[...], preferred_element_type=F32).astype(o_ref.dtype)


def _in_call(x, g, mod, w_in_r, seq):
    t, d = x.shape
    tiles_per_seq = seq // TM_IN
    return pl.pallas_call(
        _in_kernel,
        grid=(t // TM_IN, N_PROJ // TN_IN),
        in_specs=[
            pl.BlockSpec((TM_IN, d), lambda i, j: (i, 0)),
            pl.BlockSpec((1, d), lambda i, j: (0, 0)),
            pl.BlockSpec((1, N_MOD, d), lambda i, j: (i // tiles_per_seq, 0, 0)),
            pl.BlockSpec((d, TN_IN), lambda i, j: (0, j)),
        ],
        out_specs=pl.BlockSpec((TM_IN, TN_IN), lambda i, j: (i, j)),
        out_shape=jax.ShapeDtypeStruct((t, N_PROJ), BF16),
        scratch_shapes=[pltpu.VMEM((TM_IN, d), BF16)],
        compiler_params=_cparams(("parallel", "arbitrary")),
        name="in_proj",
    )(x, g, mod, w_in_r)


def _rope128(t, cos, sin_signed):
    lane = lax.broadcasted_iota(jnp.int32, t.shape, 1)
    half = QK_ROPE // 2
    partner = jnp.where(lane < half, pltpu.roll(t, LANES - half, 1), pltpu.roll(t, half, 1))
    return t * cos + partner * sin_signed


def _qkv_kernel(cq_ref, ckv_ref, kpe_ref, cos_ref, sin_ref, gq_ref, gkv_ref,
                wq_ref, wkv_ref, q_ref, k_ref, v_ref):
    cos = cos_ref[...]
    sin = sin_ref[...]
    qn = (_rms(cq_ref[...].astype(F32)) * gq_ref[...]).astype(BF16)
    q = jnp.dot(qn, wq_ref[...], preferred_element_type=F32) * (QK_HEAD ** -0.5 * LOG2_E)
    cn = (_rms(ckv_ref[...].astype(F32)) * gkv_ref[...]).astype(BF16)
    kv = jnp.dot(cn, wkv_ref[...], preferred_element_type=F32).astype(BF16)
    kr = _rope128(kpe_ref[...].astype(F32), cos, sin).astype(BF16)
    lane = lax.broadcasted_iota(jnp.int32, kr.shape, 1)
    ones_col = jnp.where(lane == 0, 1.0, 0.0).astype(BF16)
    for h in range(N_HEADS):
        base = h * HEAD_PAD
        v_ref[:, base:base + V_HEAD] = kv[:, base + QK_NOPE:base + HEAD_PAD]
        v_ref[:, base + V_HEAD:base + HEAD_PAD] = ones_col
        q_ref[:, base:base + LANES] = q[:, base:base + LANES].astype(BF16)
        q_ref[:, base + LANES:base + HEAD_PAD] = _rope128(
            q[:, base + LANES:base + HEAD_PAD], cos, sin).astype(BF16)
        k_ref[:, base:base + LANES] = kv[:, base:base + QK_NOPE]
        k_ref[:, base + LANES:base + HEAD_PAD] = kr


def _qkv_call(proj, cos, sin, gq, gkv, wq, wkv):
    t = proj.shape[0]
    tm = TM_QKV
    return pl.pallas_call(
        _qkv_kernel,
        grid=(t // tm,),
        in_specs=[
            pl.BlockSpec((tm, Q_LORA), lambda i: (i, COL_CQ // Q_LORA)),
            pl.BlockSpec((tm, KV_LORA), lambda i: (i, COL_CKV // KV_LORA)),
            pl.BlockSpec((tm, LANES), lambda i: (i, COL_KPE // LANES)),
            pl.BlockSpec((tm, LANES), lambda i: (i, 0)),
            pl.BlockSpec((tm, LANES), lambda i: (i, 0)),
            pl.BlockSpec((1, Q_LORA), lambda i: (0, 0)),
            pl.BlockSpec((1, KV_LORA), lambda i: (0, 0)),
            pl.BlockSpec(wq.shape, lambda i: (0, 0)),
            pl.BlockSpec(wkv.shape, lambda i: (0, 0)),
        ],
        out_specs=[
            pl.BlockSpec((tm, N_HEADS * HEAD_PAD), lambda i: (i, 0)),
            pl.BlockSpec((tm, N_HEADS * HEAD_PAD), lambda i: (i, 0)),
            pl.BlockSpec((tm, N_HEADS * HEAD_PAD), lambda i: (i, 0)),
        ],
        out_shape=[
            jax.ShapeDtypeStruct((t, N_HEADS * HEAD_PAD), BF16),
            jax.ShapeDtypeStruct((t, N_HEADS * HEAD_PAD), BF16),
            jax.ShapeDtypeStruct((t, N_HEADS * HEAD_PAD), BF16),
        ],
        compiler_params=_cparams(("parallel",)),
        name="qkv_build",
    )(proj, proj, proj, cos, sin, gq, gkv, wq, wkv)


def _attn_kernel(q_ref, k_ref, v_ref, o_ref):
    tq = q_ref.shape[0]
    n_chunks = k_ref.shape[0] // TK_ATTN
    for hh in range(HEADS_PER_STEP):
        qk = slice(hh * HEAD_PAD, (hh + 1) * HEAD_PAD)
        vo = slice(hh * V_HEAD, (hh + 1) * V_HEAD)
        q = q_ref[:, qk]
        m = jnp.full((tq, 1), -jnp.inf, F32)
        acc = jnp.zeros((tq, HEAD_PAD), F32)
        for j in range(n_chunks):
            rows = slice(j * TK_ATTN, (j + 1) * TK_ATTN)
            s = lax.dot_general(q, k_ref[rows, qk], (((1,), (1,)), ((), ())),
                                preferred_element_type=F32)
            m_new = jnp.maximum(m, jnp.max(s, axis=-1, keepdims=True))
            alpha = jnp.exp2(m - m_new)
            p = jnp.exp2(s - m_new)
            acc = alpha * acc + jnp.dot(p.astype(BF16), v_ref[rows, qk],
                                        preferred_element_type=F32)
            m = m_new
        o_ref[:, vo] = (acc[:, :V_HEAD] / acc[:, V_HEAD:V_HEAD + 1]).astype(o_ref.dtype)


def _attn_call(q, k, v, batch, seq):
    t = q.shape[0]
    nq = seq // TQ_ATTN
    hs = HEADS_PER_STEP
    return pl.pallas_call(
        _attn_kernel,
        grid=(batch, N_HEADS // hs, nq),
        in_specs=[
            pl.BlockSpec((TQ_ATTN, hs * HEAD_PAD), lambda b, h, i: (b * nq + i, h)),
            pl.BlockSpec((seq, hs * HEAD_PAD), lambda b, h, i: (b, h)),
            pl.BlockSpec((seq, hs * HEAD_PAD), lambda b, h, i: (b, h)),
        ],
        out_specs=pl.BlockSpec((TQ_ATTN, hs * V_HEAD), lambda b, h, i: (b * nq + i, h)),
        out_shape=jax.ShapeDtypeStruct((t, N_HEADS * V_HEAD), BF16),
        compiler_params=_cparams(("parallel", "parallel", "arbitrary")),
        name="mla_attention",
    )(q, k, v)


def _merge_kernel(o_ref, la_ref, lc_ref, gb_ref, gc_ref, u_ref, gcp_ref, up_ref, gcn_ref, un_ref,
                  x_ref, mod_ref, cw_ref, wo_ref, wco_ref, wout_ref, out_ref, *, tiles_per_seq):
    i = pl.program_id(0)
    tm = x_ref.shape[0]
    pos = i % tiles_per_seq
    has_prev = (pos != 0).astype(F32)
    has_next = (pos != tiles_per_seq - 1).astype(F32)
    z = gc_ref[...].astype(F32) * u_ref[...].astype(F32)
    last = BF16_SUBLANES - 1
    z_before = gcp_ref[last:last + 1, :].astype(F32) * up_ref[last:last + 1, :].astype(F32) * has_prev
    z_after = gcn_ref[0:1, :].astype(F32) * un_ref[0:1, :].astype(F32) * has_next
    row = lax.broadcasted_iota(jnp.int32, z.shape, 0)
    z_prev = jnp.where(row == 0, z_before, pltpu.roll(z, 1, 0))
    z_next = jnp.where(row == tm - 1, z_after, pltpu.roll(z, tm - 1, 0))
    cw = cw_ref[...]
    conv = cw[0:1] * z_prev + cw[1:2] * z + cw[2:3] * z_next
    y_conv = jnp.dot((gb_ref[...].astype(F32) * conv).astype(BF16), wco_ref[...],
                     preferred_element_type=F32)
    y_attn = jnp.dot(o_ref[...], wo_ref[...], preferred_element_type=F32)
    mix = (jax.nn.sigmoid(la_ref[...].astype(F32)) * y_attn
           + jax.nn.sigmoid(lc_ref[...].astype(F32)) * y_conv)
    y = jnp.dot(mix.astype(BF16), wout_ref[...], preferred_element_type=F32)
    out_ref[...] = x_ref[...] + mod_ref[0][2:3] * y


def _merge_call(o, proj, x, mod, conv_w, w_o, w_co, w_out, seq):
    t, d = x.shape
    tm = TM_MERGE
    tiles_per_seq = seq // tm
    hb = tm // BF16_SUBLANES
    n_halo = t // BF16_SUBLANES
    cw_blocks = CONV_WIDTH
    const = dict(pipeline_mode=pl.Buffered(1))
    prev_map = lambda c: (lambda i: (jnp.maximum(i * hb - 1, 0), c))
    next_map = lambda c: (lambda i: (jnp.minimum((i + 1) * hb, n_halo - 1), c))
    return pl.pallas_call(
        functools.partial(_merge_kernel, tiles_per_seq=tiles_per_seq),
        grid=(t // tm,),
        in_specs=[
            pl.BlockSpec((tm, d), lambda i: (i, 0)),
            pl.BlockSpec((tm, d), lambda i: (i, COL_LA // d)),
            pl.BlockSpec((tm, d), lambda i: (i, COL_LC // d)),
            pl.BlockSpec((tm, cw_blocks), lambda i: (i, COL_GB // cw_blocks)),
            pl.BlockSpec((tm, cw_blocks), lambda i: (i, COL_GC // cw_blocks)),
            pl.BlockSpec((tm, cw_blocks), lambda i: (i, COL_U // cw_blocks)),
            pl.BlockSpec((BF16_SUBLANES, cw_blocks), prev_map(COL_GC // cw_blocks)),
            pl.BlockSpec((BF16_SUBLANES, cw_blocks), prev_map(COL_U // cw_blocks)),
            pl.BlockSpec((BF16_SUBLANES, cw_blocks), next_map(COL_GC // cw_blocks)),
            pl.BlockSpec((BF16_SUBLANES, cw_blocks), next_map(COL_U // cw_blocks)),
            pl.BlockSpec((tm, d), lambda i: (i, 0)),
            pl.BlockSpec((1, N_MOD, d), lambda i: (i // tiles_per_seq, 0, 0)),
            pl.BlockSpec(conv_w.shape, lambda i: (0, 0)),
            pl.BlockSpec(w_o.shape, lambda i: (0, 0), **const),
            pl.BlockSpec(w_co.shape, lambda i: (0, 0), **const),
            pl.BlockSpec(w_out.shape, lambda i: (0, 0), **const),
        ],
        out_specs=pl.BlockSpec((tm, d), lambda i: (i, 0)),
        out_shape=jax.ShapeDtypeStruct((t, d), F32),
        compiler_params=_cparams(("parallel",)),
        name="conv_merge_out",
    )(o, proj, proj, proj, proj, proj, proj, proj, proj, proj, x, mod, conv_w, w_o, w_co, w_out)


def _first_max(vals, iota, sentinel):
    m = jnp.max(vals, axis=0, keepdims=True)
    first = jnp.min(jnp.where(vals == m, iota, sentinel), axis=0, keepdims=True)
    return m, first, iota == first


def _route_kernel(x_ref, g_ref, mod_ref, wr_ref, br_ref, w1_ref, w3_ref, w2_ref, tri_ref,
                  h_ref, xp_ref, idx_ref, wts_ref, rank_ref, cnt_ref, run_ref):
    @pl.when(pl.program_id(0) == 0)
    def _():
        run_ref[...] = jnp.zeros_like(run_ref)

    m = mod_ref[0]
    x = x_ref[...]
    h = (_rms(x) * g_ref[...]) * (1.0 + m[4:5]) + m[3:4]
    half = h.shape[1] // 2
    h_ref[...] = _pack_pair(h[:, :half], h[:, half:])
    hb = h.astype(BF16)
    a = jnp.dot(hb, w1_ref[...], preferred_element_type=F32)
    b = jnp.dot(hb, w3_ref[...], preferred_element_type=F32)
    shared = jnp.dot((_silu(a) * b).astype(BF16), w2_ref[...], preferred_element_type=F32)
    xp_ref[...] = x + m[5:6] * shared

    logits = lax.dot_general(wr_ref[...], h, (((1,), (1,)), ((), ())),
                             precision=lax.Precision.HIGHEST, preferred_element_type=F32)
    scores = jax.nn.sigmoid(logits)
    biased = scores + br_ref[...]
    tm = scores.shape[1]
    neg = -jnp.inf
    g = EXPERTS_PER_GROUP
    iota_g = lax.broadcasted_iota(jnp.int32, (g, tm), 0).astype(F32)
    group_rows = []
    for gi in range(N_GROUPS):
        blk = biased[gi * g:(gi + 1) * g, :]
        m1, _, hit = _first_max(blk, iota_g, g)
        m2 = jnp.max(jnp.where(hit, neg, blk), axis=0, keepdims=True)
        group_rows.append(m1 + m2)
    gs = jnp.concatenate(group_rows, axis=0)
    iota_n = lax.broadcasted_iota(jnp.int32, (N_GROUPS, tm), 0).astype(F32)
    chosen = jnp.zeros((N_GROUPS, tm), F32)
    for _ in range(TOPK_GROUPS):
        _, _, hit = _first_max(gs, iota_n, N_GROUPS)
        chosen = jnp.where(hit, 1.0, chosen)
        gs = jnp.where(hit, neg, gs)
    allowed = jnp.concatenate(
        [jnp.broadcast_to(chosen[gi:gi + 1, :], (g, tm)) for gi in range(N_GROUPS)], axis=0)
    cand = jnp.where(allowed > 0.0, biased, neg)
    iota_e = lax.broadcasted_iota(jnp.int32, (N_EXPERTS, tm), 0).astype(F32)

    run = run_ref[:, 0:1]
    tri = tri_ref[...]
    idx_rows, w_rows, rank_rows = [], [], []
    for _ in range(TOP_K):
        _, first, hit = _first_max(cand, iota_e, N_EXPERTS)
        idx_rows.append(first)
        w_rows.append(jnp.sum(jnp.where(hit, scores, 0.0), axis=0, keepdims=True))
        cand = jnp.where(hit, neg, cand)
        prefix = jnp.dot(jnp.where(hit, 1.0, 0.0).astype(BF16), tri, preferred_element_type=F32)
        rank_rows.append(jnp.sum(jnp.where(hit, run + prefix - 1.0, 0.0), axis=0, keepdims=True))
        run = run + prefix[:, tm - 1:tm]
    w = jnp.concatenate(w_rows, axis=0)
    idx_ref[...] = jnp.concatenate(idx_rows, axis=0).astype(jnp.int32)
    wts_ref[...] = w / jnp.sum(w, axis=0, keepdims=True) * ROUTED_SCALE
    rank_ref[...] = jnp.concatenate(rank_rows, axis=0).astype(jnp.int32)
    run_full = jnp.broadcast_to(run, run_ref.shape)
    run_ref[...] = run_full
    cnt_ref[...] = run_full


def _route_call(x, g, mod, wr_t, br_col, w1s, w3s, w2s, seq):
    t, d = x.shape
    tm = TM_ROUTE
    tiles_per_seq = seq // tm
    tri = (jnp.arange(tm)[:, None] <= jnp.arange(tm)[None, :]).astype(BF16)
    return pl.pallas_call(
        _route_kernel,
        grid=(t // tm,),
        in_specs=[
            pl.BlockSpec((tm, d), lambda i: (i, 0)),
            pl.BlockSpec((1, d), lambda i: (0, 0)),
            pl.BlockSpec((1, N_MOD, d), lambda i: (i // tiles_per_seq, 0, 0)),
            pl.BlockSpec(wr_t.shape, lambda i: (0, 0)),
            pl.BlockSpec(br_col.shape, lambda i: (0, 0)),
            pl.BlockSpec(w1s.shape, lambda i: (0, 0)),
            pl.BlockSpec(w3s.shape, lambda i: (0, 0)),
            pl.BlockSpec(w2s.shape, lambda i: (0, 0)),
            pl.BlockSpec((tm, tm), lambda i: (0, 0)),
        ],
        out_specs=[
            pl.BlockSpec((tm, d // 2), lambda i: (i, 0)),
            pl.BlockSpec((tm, d), lambda i: (i, 0)),
            pl.BlockSpec((TOP_K, tm), lambda i: (0, i)),
            pl.BlockSpec((TOP_K, tm), lambda i: (0, i)),
            pl.BlockSpec((TOP_K, tm), lambda i: (0, i)),
            pl.BlockSpec((N_EXPERTS, LANES), lambda i: (0, 0)),
        ],
        out_shape=[
            jax.ShapeDtypeStruct((t, d // 2), jnp.uint32),
            jax.ShapeDtypeStruct((t, d), F32),
            jax.ShapeDtypeStruct((TOP_K, t), jnp.int32),
            jax.ShapeDtypeStruct((TOP_K, t), F32),
            jax.ShapeDtypeStruct((TOP_K, t), jnp.int32),
            jax.ShapeDtypeStruct((N_EXPERTS, LANES), F32),
        ],
        scratch_shapes=[pltpu.VMEM((N_EXPERTS, LANES), F32)],
        compiler_params=_cparams(("arbitrary",)),
        name="moe_route",
    )(x, g, mod, wr_t, br_col, w1s, w3s, w2s, tri)


def _row_plan(idx_t, rank_t, cnt, n_tok):
    tm = TM_EXP
    n_tiles = TOP_K * n_tok // tm + N_EXPERTS
    counts = cnt[:, 0].astype(jnp.int32)
    padded = (counts + tm - 1) // tm * tm
    pend = jnp.cumsum(padded)
    pstart = pend - padded
    tile_start = jnp.arange(n_tiles, dtype=jnp.int32) * tm
    tile_e = jnp.minimum(jnp.sum((pend[None, :] <= tile_start[:, None]).astype(jnp.int32), axis=1),
                         N_EXPERTS - 1).astype(jnp.int32)
    nact = (pend[-1] // tm).astype(jnp.int32).reshape(1)
    eids = jnp.arange(N_EXPERTS, dtype=jnp.int32)
    nonempty = counts > 0
    after = eids[None, :] > eids[:, None]
    ordinal = jnp.sum((~after & nonempty[None, :]).astype(jnp.int32), axis=1) - 1
    next_e = jnp.min(jnp.where(after & nonempty[None, :], eids[None, :], N_EXPERTS), axis=1)
    tile_hot = tile_e[:, None] == eids[None, :]
    lookup = lambda table: jnp.sum(jnp.where(tile_hot, table[None, :], 0), axis=1)
    is_first = (tile_start == lookup(pstart)).astype(jnp.int32)
    tile_info = jnp.stack([tile_e, is_first, lookup(ordinal) % 2, lookup(next_e)], axis=0)
    zero_start = jnp.maximum(pend - tm, 0).astype(jnp.int32)
    experts = jnp.arange(N_EXPERTS, dtype=jnp.int32)[:, None, None]
    dest = rank_t + jnp.sum(jnp.where(idx_t[None] == experts, pstart[:, None, None], 0), axis=0)
    nt = n_tok // TM_TOK
    dest_tiles = dest.reshape(TOP_K, nt, TM_TOK).transpose(1, 0, 2).reshape(nt, 1, TOP_K * TM_TOK)
    return tile_info.astype(jnp.int32), nact, zero_start, dest_tiles.astype(jnp.int32), n_tiles * tm


def _dispatch_kernel(zs_ref, dest_ref, h_ref, xs_hbm, zbuf, zsem, sem):
    i = pl.program_id(0)
    tm = TM_TOK
    tme = zbuf.shape[0]

    def row_copy(j, dst_row):
        return pltpu.make_async_copy(h_ref.at[pl.ds(j, 1)], xs_hbm.at[pl.ds(dst_row, 1)], sem)

    @pl.when(i == 0)
    def _():
        zbuf[...] = jnp.zeros_like(zbuf)

        def zero_copy(e):
            start = pl.multiple_of(zs_ref[e], tme)
            return pltpu.make_async_copy(zbuf, xs_hbm.at[pl.ds(start, tme)], zsem)
        for e in range(N_EXPERTS):
            zero_copy(e).start()
        for e in range(N_EXPERTS):
            zero_copy(e).wait()

    for j in range(tm):
        for k in range(TOP_K):
            row_copy(j, dest_ref[0, 0, k * tm + j]).start(priority=k % 2)

    for j in range(tm):
        for _ in range(TOP_K):
            row_copy(j, 0).wait()


def _dispatch_call(zero_start, dest_tiles, hp, n_rows):
    t, dp = hp.shape
    nt = t // TM_TOK
    grid_spec = pltpu.PrefetchScalarGridSpec(
        num_scalar_prefetch=1,
        grid=(nt,),
        in_specs=[
            pl.BlockSpec((1, 1, TOP_K * TM_TOK), lambda i, zs: (i, 0, 0), memory_space=pltpu.SMEM),
            pl.BlockSpec((TM_TOK, dp), lambda i, zs: (i, 0)),
        ],
        out_specs=pl.BlockSpec(memory_space=pl.ANY),
        scratch_shapes=[
            pltpu.VMEM((TM_EXP, dp), hp.dtype),
            pltpu.SemaphoreType.DMA(()),
            pltpu.SemaphoreType.DMA(()),
        ],
    )
    return pl.pallas_call(
        _dispatch_kernel,
        grid_spec=grid_spec,
        out_shape=jax.ShapeDtypeStruct((n_rows, dp), hp.dtype),
        compiler_params=_cparams(("arbitrary",)),
        name="moe_dispatch",
    )(zero_start, dest_tiles, hp)


def _expert_kernel(ti_ref, nact_ref, x_ref, w1_hbm, w3_hbm, w2_hbm, y_ref,
                   w1f, w3f, w2f, w1b, w3b, w2b, wsem, *, layer):
    i = pl.program_id(0)

    def weight_copies(e, s):
        return (pltpu.make_async_copy(w1_hbm.at[layer, e], w1f.at[s], wsem.at[s]),
                pltpu.make_async_copy(w3_hbm.at[layer, e], w3f.at[s], wsem.at[s]),
                pltpu.make_async_copy(w2_hbm.at[layer, e], w2f.at[s], wsem.at[s]))

    @pl.when(i < nact_ref[0])
    def _():
        @pl.when(ti_ref[1, i] == 1)
        def _():
            e = ti_ref[0, i]
            nxt = ti_ref[3, i]
            for s in range(2):
                @pl.when(ti_ref[2, i] == s)
                def _():
                    @pl.when(i == 0)
                    def _():
                        for c in weight_copies(e, s):
                            c.start()
                    for c in weight_copies(e, s):
                        c.wait()

                    @pl.when(nxt < N_EXPERTS)
                    def _():
                        for c in weight_copies(nxt, 1 - s):
                            c.start()
                    w1b[...] = w1f[s].astype(BF16)
                    w3b[...] = w3f[s].astype(BF16)
                    w2b[...] = w2f[s].astype(BF16)

        half = x_ref.shape[1]
        x_lo, x_hi = _unpack_pair(x_ref[...])
        x_lo = x_lo.astype(BF16)
        x_hi = x_hi.astype(BF16)
        a = (jnp.dot(x_lo, w1b[:half, :], preferred_element_type=F32)
             + jnp.dot(x_hi, w1b[half:, :], preferred_element_type=F32))
        b = (jnp.dot(x_lo, w3b[:half, :], preferred_element_type=F32)
             + jnp.dot(x_hi, w3b[half:, :], preferred_element_type=F32))
        g = (_silu(a) * b).astype(BF16)
        y_ref[...] = _pack_pair(jnp.dot(g, w2b[:, :half], preferred_element_type=F32),
                                jnp.dot(g, w2b[:, half:], preferred_element_type=F32))


def _expert_call(layer, tile_info, nact, xs, w1_e, w3_e, w2_e):
    tm = TM_EXP
    n_rows, dp = xs.shape
    d = 2 * dp
    n_tiles = n_rows // tm
    de = w1_e.shape[-1]
    row_map = lambda i, ti, na: (jnp.minimum(i, na[0] - 1), 0)
    grid_spec = pltpu.PrefetchScalarGridSpec(
        num_scalar_prefetch=2,
        grid=(n_tiles,),
        in_specs=[
            pl.BlockSpec((tm, dp), row_map),
            pl.BlockSpec(memory_space=pl.ANY),
            pl.BlockSpec(memory_space=pl.ANY),
            pl.BlockSpec(memory_space=pl.ANY),
        ],
        out_specs=pl.BlockSpec((tm, dp), row_map),
        scratch_shapes=[
            pltpu.VMEM((2, d, de), F32),
            pltpu.VMEM((2, d, de), F32),
            pltpu.VMEM((2, de, d), F32),
            pltpu.VMEM((d, de), BF16),
            pltpu.VMEM((d, de), BF16),
            pltpu.VMEM((de, d), BF16),
            pltpu.SemaphoreType.DMA((2,)),
        ],
    )
    return pl.pallas_call(
        functools.partial(_expert_kernel, layer=layer),
        grid_spec=grid_spec,
        out_shape=jax.ShapeDtypeStruct((n_rows, dp), xs.dtype),
        compiler_params=_cparams(("arbitrary",)),
        name="moe_experts",
    )(tile_info, nact, xs, w1_e, w3_e, w2_e)


def _combine_kernel(dnext_ref, dfirst_ref, y_hbm, w_ref, xp_ref, mod_ref, gf_ref, out_ref,
                    ybuf0, ybuf1, sem, *, final):
    i = pl.program_id(0)
    n = pl.num_programs(0)
    tm = TM_TOK
    slot = i % 2
    ybufs = (ybuf0, ybuf1)

    def row_copy(src_row, k, j, s):
        return pltpu.make_async_copy(y_hbm.at[pl.ds(src_row, 1)], ybufs[s].at[k, pl.ds(j, 1)],
                                     sem.at[s])

    def start_rows_rolled(d_ref, s):
        def body(j, carry):
            for k in range(TOP_K):
                row_copy(d_ref[0, 0, k * tm + j], k, j, s).start(priority=k % 2)
            return carry
        lax.fori_loop(0, tm, body, 0, unroll=DMA_UNROLL)

    def start_rows(d_ref, s):
        for j in range(tm):
            for k in range(TOP_K):
                row_copy(d_ref[0, 0, k * tm + j], k, j, s).start(priority=k % 2)

    def wait_rows(s):
        for j in range(tm):
            for k in range(TOP_K):
                row_copy(0, k, j, s).wait()

    @pl.when(i == 0)
    def _():
        start_rows_rolled(dfirst_ref, 0)

    def reduce_rows(s):
        ybuf = ybufs[s]
        w = w_ref[...]
        half = ybuf.shape[2]
        acc_lo, acc_hi = _unpack_pair(ybuf[0])
        acc_lo = acc_lo * w[:, 0:1]
        acc_hi = acc_hi * w[:, 0:1]
        for k in range(1, TOP_K):
            y_lo, y_hi = _unpack_pair(ybuf[k])
            acc_lo = acc_lo + y_lo * w[:, k:k + 1]
            acc_hi = acc_hi + y_hi * w[:, k:k + 1]
        gate = mod_ref[0][5:6]
        x_lo = xp_ref[:, :half] + gate[:, :half] * acc_lo
        x_hi = xp_ref[:, half:] + gate[:, half:] * acc_hi
        if final:
            ms = (jnp.sum(x_lo * x_lo, axis=-1, keepdims=True)
                  + jnp.sum(x_hi * x_hi, axis=-1, keepdims=True)) / (2 * half)
            r = lax.rsqrt(ms + EPS)
            x_lo = x_lo * r * gf_ref[:, :half]
            x_hi = x_hi * r * gf_ref[:, half:]
        out_ref[:, :half] = x_lo
        out_ref[:, half:] = x_hi

    for s in range(2):
        @pl.when((slot == s) & (i + 1 < n))
        def _():
            wait_rows(s)
            start_rows(dnext_ref, 1 - s)
            reduce_rows(s)

        @pl.when((slot == s) & (i + 1 == n))
        def _():
            wait_rows(s)
            reduce_rows(s)


def _combine_call(dest_tiles, ys, wts_col, xp, mod, g_final, seq, final):
    t, d = xp.shape
    tm = TM_TOK
    nt = t // tm
    tiles_per_seq = seq // tm
    return pl.pallas_call(
        functools.partial(_combine_kernel, final=final),
        grid=(nt,),
        in_specs=[
            pl.BlockSpec((1, 1, TOP_K * tm), lambda i: (jnp.minimum(i + 1, nt - 1), 0, 0),
                         memory_space=pltpu.SMEM),
            pl.BlockSpec((1, 1, TOP_K * tm), lambda i: (0, 0, 0), memory_space=pltpu.SMEM),
            pl.BlockSpec(memory_space=pl.ANY),
            pl.BlockSpec((tm, TOP_K), lambda i: (i, 0)),
            pl.BlockSpec((tm, d), lambda i: (i, 0)),
            pl.BlockSpec((1, N_MOD, d), lambda i: (i // tiles_per_seq, 0, 0)),
            pl.BlockSpec((1, d), lambda i: (0, 0)),
        ],
        out_specs=pl.BlockSpec((tm, d), lambda i: (i, 0)),
        out_shape=jax.ShapeDtypeStruct((t, d), F32),
        scratch_shapes=[
            pltpu.VMEM((TOP_K, tm, ys.shape[1]), ys.dtype),
            pltpu.VMEM((TOP_K, tm, ys.shape[1]), ys.dtype),
            pltpu.SemaphoreType.DMA((2,)),
        ],
        compiler_params=_cparams(("arbitrary",)),
        name="moe_combine",
    )(dest_tiles, dest_tiles, ys, wts_col, xp, mod, g_final)


def _reorder_w_in(w):
    cq, ckv, kpe, gb, gc, u, la, lc = jnp.split(
        w, [512, 768, 832, 1856, 2880, 3904, 5952], axis=1)
    pad = jnp.zeros((w.shape[0], N_PROJ - 8000), w.dtype)
    return jnp.concatenate([la, lc, gb, gc, u, cq, ckv, kpe, pad], axis=1).astype(BF16)


def kernel(x, c, positions, w_ada, b_ada, g_mix, w_in, g_q, w_uq, g_kv, w_ukv, w_o_attn, conv_w,
           w_conv_out, w_out, g_ffn, w_router, b_router, w1_e, w3_e, w2_e, w1_s, w3_s, w2_s, g_final):
    batch, seq, d = x.shape
    depth = w_ada.shape[0]
    t = batch * seq
    xf = x.reshape(t, d)

    c8 = jnp.zeros((8, d), F32).at[:batch].set(c)
    mod_all = _ada_call(c8, w_ada, b_ada.reshape(depth, 1, N_MOD * d))

    inv = 1.0 / (ROPE_THETA ** (jnp.arange(0, QK_ROPE, 2, dtype=F32) / QK_ROPE))
    inv128 = jnp.concatenate([inv, inv, jnp.zeros((LANES - QK_ROPE,), F32)]).reshape(1, LANES)
    cos_t, sin_t = _rope_call(positions.reshape(t, 1), inv128)

    for l in range(depth):
        mod = mod_all[l, :batch].reshape(batch, N_MOD, d)
        wq = jnp.pad(w_uq[l], ((0, 0), (0, 0), (0, HEAD_PAD - QK_HEAD))).reshape(
            Q_LORA, N_HEADS * HEAD_PAD).astype(BF16)
        wkv = w_ukv[l].reshape(KV_LORA, N_HEADS * (QK_NOPE + V_HEAD)).astype(BF16)

        proj = _in_call(xf, g_mix[l].reshape(1, d), mod, _reorder_w_in(w_in[l]), seq)
        q, k, v = _qkv_call(proj, cos_t, sin_t, g_q[l].reshape(1, Q_LORA),
                            g_kv[l].reshape(1, KV_LORA), wq, wkv)
        o = _attn_call(q, k, v, batch, seq)
        xf = _merge_call(o, proj, xf, mod, conv_w[l].reshape(3, CONV_WIDTH),
                         w_o_attn[l].astype(BF16), w_conv_out[l].astype(BF16),
                         w_out[l].astype(BF16), seq)

        h, xp, idx_t, wts_t, rank_t, cnt = _route_call(
            xf, g_ffn[l].reshape(1, d), mod, w_router[l].T, b_router[l].reshape(N_EXPERTS, 1),
            w1_s[l].astype(BF16), w3_s[l].astype(BF16), w2_s[l].astype(BF16), seq)
        tile_info, nact, zero_start, dest_tiles, n_rows = _row_plan(idx_t, rank_t, cnt, t)
        xs = _dispatch_call(zero_start, dest_tiles, h, n_rows)
        ys = _expert_call(l, tile_info, nact, xs, w1_e, w3_e, w2_e)
        xf = _combine_call(dest_tiles, ys, wts_t.T, xp, mod, g_final.reshape(1, d), seq,
                           final=(l == depth - 1))

    return xf.reshape(batch, seq, d)
```

```python
import functools

import jax
import jax.numpy as jnp
from jax import lax
from jax.experimental import pallas as pl
from jax.experimental.pallas import tpu as pltpu

F32 = jnp.float32
BF16 = jnp.bfloat16

N_HEADS = 16
Q_LORA = 512
KV_LORA = 256
QK_NOPE = 128
QK_ROPE = 64
V_HEAD = 128
QK_HEAD = QK_NOPE + QK_ROPE
ROPE_THETA = 10000.0
CONV_WIDTH = 1024
N_EXPERTS = 64
TOP_K = 8
N_GROUPS = 8
TOPK_GROUPS = 4
EXPERTS_PER_GROUP = N_EXPERTS // N_GROUPS
ROUTED_SCALE = 2.5
N_MOD = 6
EPS = 1e-6
LOG2_E = 1.4426950408889634

LANES = 128
BF16_SUBLANES = 16
HEAD_PAD = 2 * LANES

COL_LA = 0
COL_LC = 2048
COL_GB = 4096
COL_GC = 5120
COL_U = 6144
COL_CQ = 7168
COL_CKV = 7680
COL_KPE = 7936
N_PROJ = 8064

VMEM_LIMIT = 56 * 1024 * 1024

TM_IN = 1024
TN_IN = 1152
TM_QKV = 512
TQ_ATTN = 1024
HEADS_PER_STEP = 1
TK_ATTN = 512
TM_MERGE = 256
TM_ROUTE = 512
TM_EXP = 512
TM_TOK = 128
TN_ADA = 2048
DMA_UNROLL = 8


def _cparams(sem):
    return pltpu.CompilerParams(dimension_semantics=sem, vmem_limit_bytes=VMEM_LIMIT)


def _rms(x):
    return x * lax.rsqrt(jnp.mean(x * x, axis=-1, keepdims=True) + EPS)


def _silu(x):
    return x * jax.nn.sigmoid(x)


HI_HALF = 0xFFFF0000


def _pack_pair(lo, hi):
    lo_bits = lax.bitcast_convert_type(lo.astype(BF16).astype(F32), jnp.uint32) >> 16
    hi_bits = lax.bitcast_convert_type(hi.astype(BF16).astype(F32), jnp.uint32) & jnp.uint32(HI_HALF)
    return hi_bits | lo_bits


def _unpack_pair(u):
    lo = lax.bitcast_convert_type(u << 16, F32)
    hi = lax.bitcast_convert_type(u & jnp.uint32(HI_HALF), F32)
    return lo, hi


def _ada_kernel(c_ref, w_ref, b_ref, o_ref):
    ca = _silu(c_ref[...]).astype(BF16)
    o_ref[0] = jnp.dot(ca, w_ref[0].astype(BF16), preferred_element_type=F32) + b_ref[0]


def _ada_call(c8, w_ada, b_ada3):
    depth, d, n = w_ada.shape
    return pl.pallas_call(
        _ada_kernel,
        grid=(depth, n // TN_ADA),
        in_specs=[
            pl.BlockSpec((8, d), lambda l, j: (0, 0)),
            pl.BlockSpec((1, d, TN_ADA), lambda l, j: (l, 0, j)),
            pl.BlockSpec((1, 1, TN_ADA), lambda l, j: (l, 0, j)),
        ],
        out_specs=pl.BlockSpec((1, 8, TN_ADA), lambda l, j: (l, 0, j)),
        out_shape=jax.ShapeDtypeStruct((depth, 8, n), F32),
        compiler_params=_cparams(("parallel", "parallel")),
        name="ada_mod",
    )(c8, w_ada, b_ada3)


def _rope_kernel(pos_ref, inv_ref, cos_ref, sin_ref):
    ang = pos_ref[...].astype(F32) * inv_ref[...]
    lane = lax.broadcasted_iota(jnp.int32, ang.shape, 1)
    half = QK_ROPE // 2
    cos_ref[...] = jnp.where(lane < QK_ROPE, jnp.cos(ang), 0.0)
    s = jnp.sin(ang)
    sin_ref[...] = jnp.where(lane < half, -s, jnp.where(lane < QK_ROPE, s, 0.0))


def _rope_call(pos_col, inv128):
    t = pos_col.shape[0]
    tm = 1024
    return pl.pallas_call(
        _rope_kernel,
        grid=(t // tm,),
        in_specs=[pl.BlockSpec((tm, 1), lambda i: (i, 0)),
                  pl.BlockSpec((1, LANES), lambda i: (0, 0))],
        out_specs=[pl.BlockSpec((tm, LANES), lambda i: (i, 0)),
                   pl.BlockSpec((tm, LANES), lambda i: (i, 0))],
        out_shape=[jax.ShapeDtypeStruct((t, LANES), F32)] * 2,
        compiler_params=_cparams(("parallel",)),
        name="rope_tables",
    )(pos_col, inv128)


def _in_kernel(x_ref, g_ref, mod_ref, w_ref, o_ref, h_ref):
    @pl.when(pl.program_id(1) == 0)
    def _():
        m = mod_ref[0]
        h = _rms(x_ref[...]) * g_ref[...]
        h_ref[...] = (h * (1.0 + m[1:2]) + m[0:1]).astype(BF16)

    o_ref[...] = jnp.dot(h_ref[...], w_ref[...], preferred_element_type=F32).astype(o_ref.dtype)


def _in_call(x, g, mod, w_in_r, seq):
    t, d = x.shape
    tiles_per_seq = seq // TM_IN
    return pl.pallas_call(
        _in_kernel,
        grid=(t // TM_IN, N_PROJ // TN_IN),
        in_specs=[
            pl.BlockSpec((TM_IN, d), lambda i, j: (i, 0)),
            pl.BlockSpec((1, d), lambda i, j: (0, 0)),
            pl.BlockSpec((1, N_MOD, d), lambda i, j: (i // tiles_per_seq, 0, 0)),
            pl.BlockSpec((d, TN_IN), lambda i, j: (0, j)),
        ],
        out_specs=pl.BlockSpec((TM_IN, TN_IN), lambda i, j: (i, j)),
        out_shape=jax.ShapeDtypeStruct((t, N_PROJ), BF16),
        scratch_shapes=[pltpu.VMEM((TM_IN, d), BF16)],
        compiler_params=_cparams(("parallel", "arbitrary")),
        name="in_proj",
    )(x, g, mod, w_in_r)


def _rope128(t, cos, sin_signed):
    lane = lax.broadcasted_iota(jnp.int32, t.shape, 1)
    half = QK_ROPE // 2
    partner = jnp.where(lane < half, pltpu.roll(t, LANES - half, 1), pltpu.roll(t, half, 1))
    return t * cos + partner * sin_signed


def _qkv_kernel(cq_ref, ckv_ref, kpe_ref, cos_ref, sin_ref, gq_ref, gkv_ref,
                wq_ref, wkv_ref, q_ref, k_ref, v_ref):
    cos = cos_ref[...]
    sin = sin_ref[...]
    qn = (_rms(cq_ref[...].astype(F32)) * gq_ref[...]).astype(BF16)
    q = jnp.dot(qn, wq_ref[...], preferred_element_type=F32) * (QK_HEAD ** -0.5 * LOG2_E)
    cn = (_rms(ckv_ref[...].astype(F32)) * gkv_ref[...]).astype(BF16)
    kv = jnp.dot(cn, wkv_ref[...], preferred_element_type=F32).astype(BF16)
    kr = _rope128(kpe_ref[...].astype(F32), cos, sin).astype(BF16)
    lane = lax.broadcasted_iota(jnp.int32, kr.shape, 1)
    ones_col = jnp.where(lane == 0, 1.0, 0.0).astype(BF16)
    for h in range(N_HEADS):
        base = h * HEAD_PAD
        v_ref[:, base:base + V_HEAD] = kv[:, base + QK_NOPE:base + HEAD_PAD]
        v_ref[:, base + V_HEAD:base + HEAD_PAD] = ones_col
        q_ref[:, base:base + LANES] = q[:, base:base + LANES].astype(BF16)
        q_ref[:, base + LANES:base + HEAD_PAD] = _rope128(
            q[:, base + LANES:base + HEAD_PAD], cos, sin).astype(BF16)
        k_ref[:, base:base + LANES] = kv[:, base:base + QK_NOPE]
        k_ref[:, base + LANES:base + HEAD_PAD] = kr


def _qkv_call(proj, cos, sin, gq, gkv, wq, wkv):
    t = proj.shape[0]
    tm = TM_QKV
    return pl.pallas_call(
        _qkv_kernel,
        grid=(t // tm,),
        in_specs=[
            pl.BlockSpec((tm, Q_LORA), lambda i: (i, COL_CQ // Q_LORA)),
            pl.BlockSpec((tm, KV_LORA), lambda i: (i, COL_CKV // KV_LORA)),
            pl.BlockSpec((tm, LANES), lambda i: (i, COL_KPE // LANES)),
            pl.BlockSpec((tm, LANES), lambda i: (i, 0)),
            pl.BlockSpec((tm, LANES), lambda i: (i, 0)),
            pl.BlockSpec((1, Q_LORA), lambda i: (0, 0)),
            pl.BlockSpec((1, KV_LORA), lambda i: (0, 0)),
            pl.BlockSpec(wq.shape, lambda i: (0, 0)),
            pl.BlockSpec(wkv.shape, lambda i: (0, 0)),
        ],
        out_specs=[
            pl.BlockSpec((tm, N_HEADS * HEAD_PAD), lambda i: (i, 0)),
            pl.BlockSpec((tm, N_HEADS * HEAD_PAD), lambda i: (i, 0)),
            pl.BlockSpec((tm, N_HEADS * HEAD_PAD), lambda i: (i, 0)),
        ],
        out_shape=[
            jax.ShapeDtypeStruct((t, N_HEADS * HEAD_PAD), BF16),
            jax.ShapeDtypeStruct((t, N_HEADS * HEAD_PAD), BF16),
            jax.ShapeDtypeStruct((t, N_HEADS * HEAD_PAD), BF16),
        ],
        compiler_params=_cparams(("parallel",)),
        name="qkv_build",
    )(proj, proj, proj, cos, sin, gq, gkv, wq, wkv)


def _attn_kernel(q_ref, k_ref, v_ref, o_ref):
    tq = q_ref.shape[0]
    n_chunks = k_ref.shape[0] // TK_ATTN
    for hh in range(HEADS_PER_STEP):
        qk = slice(hh * HEAD_PAD, (hh + 1) * HEAD_PAD)
        vo = slice(hh * V_HEAD, (hh + 1) * V_HEAD)
        q = q_ref[:, qk]
        m = jnp.full((tq, 1), -jnp.inf, F32)
        acc = jnp.zeros((tq, HEAD_PAD), F32)
        for j in range(n_chunks):
            rows = slice(j * TK_ATTN, (j + 1) * TK_ATTN)
            s = lax.dot_general(q, k_ref[rows, qk], (((1,), (1,)), ((), ())),
                                preferred_element_type=F32)
            m_new = jnp.maximum(m, jnp.max(s, axis=-1, keepdims=True))
            alpha = jnp.exp2(m - m_new)
            p = jnp.exp2(s - m_new)
            acc = alpha * acc + jnp.dot(p.astype(BF16), v_ref[rows, qk],
                                        preferred_element_type=F32)
            m = m_new
        o_ref[:, vo] = (acc[:, :V_HEAD] / acc[:, V_HEAD:V_HEAD + 1]).astype(o_ref.dtype)


def _attn_call(q, k, v, batch, seq):
    t = q.shape[0]
    nq = seq // TQ_ATTN
    hs = HEADS_PER_STEP
    return pl.pallas_call(
        _attn_kernel,
        grid=(batch, N_HEADS // hs, nq),
        in_specs=[
            pl.BlockSpec((TQ_ATTN, hs * HEAD_PAD), lambda b, h, i: (b * nq + i, h)),
            pl.BlockSpec((seq, hs * HEAD_PAD), lambda b, h, i: (b, h)),
            pl.BlockSpec((seq, hs * HEAD_PAD), lambda b, h, i: (b, h)),
        ],
        out_specs=pl.BlockSpec((TQ_ATTN, hs * V_HEAD), lambda b, h, i: (b * nq + i, h)),
        out_shape=jax.ShapeDtypeStruct((t, N_HEADS * V_HEAD), BF16),
        compiler_params=_cparams(("parallel", "parallel", "arbitrary")),
        name="mla_attention",
    )(q, k, v)


def _merge_kernel(o_ref, la_ref, lc_ref, gb_ref, gc_ref, u_ref, gcp_ref, up_ref, gcn_ref, un_ref,
                  x_ref, mod_ref, cw_ref, wo_ref, wco_ref, wout_ref, out_ref, *, tiles_per_seq):
    i = pl.program_id(0)
    tm = x_ref.shape[0]
    pos = i % tiles_per_seq
    has_prev = (pos != 0).astype(F32)
    has_next = (pos != tiles_per_seq - 1).astype(F32)
    z = gc_ref[...].astype(F32) * u_ref[...].astype(F32)
    last = BF16_SUBLANES - 1
    z_before = gcp_ref[last:last + 1, :].astype(F32) * up_ref[last:last + 1, :].astype(F32) * has_prev
    z_after = gcn_ref[0:1, :].astype(F32) * un_ref[0:1, :].astype(F32) * has_next
    row = lax.broadcasted_iota(jnp.int32, z.shape, 0)
    z_prev = jnp.where(row == 0, z_before, pltpu.roll(z, 1, 0))
    z_next = jnp.where(row == tm - 1, z_after, pltpu.roll(z, tm - 1, 0))
    cw = cw_ref[...]
    conv = cw[0:1] * z_prev + cw[1:2] * z + cw[2:3] * z_next
    y_conv = jnp.dot((gb_ref[...].astype(F32) * conv).astype(BF16), wco_ref[...],
                     preferred_element_type=F32)
    y_attn = jnp.dot(o_ref[...], wo_ref[...], preferred_element_type=F32)
    mix = (jax.nn.sigmoid(la_ref[...].astype(F32)) * y_attn
           + jax.nn.sigmoid(lc_ref[...].astype(F32)) * y_conv)
    y = jnp.dot(mix.astype(BF16), wout_ref[...], preferred_element_type=F32)
    out_ref[...] = x_ref[...] + mod_ref[0][2:3] * y


def _merge_call(o, proj, x, mod, conv_w, w_o, w_co, w_out, seq):
    t, d = x.shape
    tm = TM_MERGE
    tiles_per_seq = seq // tm
    hb = tm // BF16_SUBLANES
    n_halo = t // BF16_SUBLANES
    cw_blocks = CONV_WIDTH
    const = dict(pipeline_mode=pl.Buffered(1))
    prev_map = lambda c: (lambda i: (jnp.maximum(i * hb - 1, 0), c))
    next_map = lambda c: (lambda i: (jnp.minimum((i + 1) * hb, n_halo - 1), c))
    return pl.pallas_call(
        functools.partial(_merge_kernel, tiles_per_seq=tiles_per_seq),
        grid=(t // tm,),
        in_specs=[
            pl.BlockSpec((tm, d), lambda i: (i, 0)),
            pl.BlockSpec((tm, d), lambda i: (i, COL_LA // d)),
            pl.BlockSpec((tm, d), lambda i: (i, COL_LC // d)),
            pl.BlockSpec((tm, cw_blocks), lambda i: (i, COL_GB // cw_blocks)),
            pl.BlockSpec((tm, cw_blocks), lambda i: (i, COL_GC // cw_blocks)),
            pl.BlockSpec((tm, cw_blocks), lambda i: (i, COL_U // cw_blocks)),
            pl.BlockSpec((BF16_SUBLANES, cw_blocks), prev_map(COL_GC // cw_blocks)),
            pl.BlockSpec((BF16_SUBLANES, cw_blocks), prev_map(COL_U // cw_blocks)),
            pl.BlockSpec((BF16_SUBLANES, cw_blocks), next_map(COL_GC // cw_blocks)),
            pl.BlockSpec((BF16_SUBLANES, cw_blocks), next_map(COL_U // cw_blocks)),
            pl.BlockSpec((tm, d), lambda i: (i, 0)),
            pl.BlockSpec((1, N_MOD, d), lambda i: (i // tiles_per_seq, 0, 0)),
            pl.BlockSpec(conv_w.shape, lambda i: (0, 0)),
            pl.BlockSpec(w_o.shape, lambda i: (0, 0), **const),
            pl.BlockSpec(w_co.shape, lambda i: (0, 0), **const),
            pl.BlockSpec(w_out.shape, lambda i: (0, 0), **const),
        ],
        out_specs=pl.BlockSpec((tm, d), lambda i: (i, 0)),
        out_shape=jax.ShapeDtypeStruct((t, d), F32),
        compiler_params=_cparams(("parallel",)),
        name="conv_merge_out",
    )(o, proj, proj, proj, proj, proj, proj, proj, proj, proj, x, mod, conv_w, w_o, w_co, w_out)


def _first_max(vals, iota, sentinel):
    m = jnp.max(vals, axis=0, keepdims=True)
    first = jnp.min(jnp.where(vals == m, iota, sentinel), axis=0, keepdims=True)
    return m, first, iota == first


def _route_kernel(x_ref, g_ref, mod_ref, wr_ref, br_ref, w1_ref, w3_ref, w2_ref, tri_ref,
                  h_ref, xp_ref, idx_ref, wts_ref, rank_ref, cnt_ref, run_ref):
    @pl.when(pl.program_id(0) == 0)
    def _():
        run_ref[...] = jnp.zeros_like(run_ref)

    m = mod_ref[0]
    x = x_ref[...]
    h = (_rms(x) * g_ref[...]) * (1.0 + m[4:5]) + m[3:4]
    half = h.shape[1] // 2
    h_ref[...] = _pack_pair(h[:, :half], h[:, half:])
    hb = h.astype(BF16)
    a = jnp.dot(hb, w1_ref[...], preferred_element_type=F32)
    b = jnp.dot(hb, w3_ref[...], preferred_element_type=F32)
    shared = jnp.dot((_silu(a) * b).astype(BF16), w2_ref[...], preferred_element_type=F32)
    xp_ref[...] = x + m[5:6] * shared

    logits = lax.dot_general(wr_ref[...], h, (((1,), (1,)), ((), ())),
                             precision=lax.Precision.HIGHEST, preferred_element_type=F32)
    scores = jax.nn.sigmoid(logits)
    biased = scores + br_ref[...]
    tm = scores.shape[1]
    neg = -jnp.inf
    g = EXPERTS_PER_GROUP
    iota_g = lax.broadcasted_iota(jnp.int32, (g, tm), 0).astype(F32)
    group_rows = []
    for gi in range(N_GROUPS):
        blk = biased[gi * g:(gi + 1) * g, :]
        m1, _, hit = _first_max(blk, iota_g, g)
        m2 = jnp.max(jnp.where(hit, neg, blk), axis=0, keepdims=True)
        group_rows.append(m1 + m2)
    gs = jnp.concatenate(group_rows, axis=0)
    iota_n = lax.broadcasted_iota(jnp.int32, (N_GROUPS, tm), 0).astype(F32)
    chosen = jnp.zeros((N_GROUPS, tm), F32)
    for _ in range(TOPK_GROUPS):
        _, _, hit = _first_max(gs, iota_n, N_GROUPS)
        chosen = jnp.where(hit, 1.0, chosen)
        gs = jnp.where(hit, neg, gs)
    allowed = jnp.concatenate(
        [jnp.broadcast_to(chosen[gi:gi + 1, :], (g, tm)) for gi in range(N_GROUPS)], axis=0)
    cand = jnp.where(allowed > 0.0, biased, neg)
    iota_e = lax.broadcasted_iota(jnp.int32, (N_EXPERTS, tm), 0).astype(F32)

    run = run_ref[:, 0:1]
    tri = tri_ref[...]
    idx_rows, w_rows, rank_rows = [], [], []
    for _ in range(TOP_K):
        _, first, hit = _first_max(cand, iota_e, N_EXPERTS)
        idx_rows.append(first)
        w_rows.append(jnp.sum(jnp.where(hit, scores, 0.0), axis=0, keepdims=True))
        cand = jnp.where(hit, neg, cand)
        prefix = jnp.dot(jnp.where(hit, 1.0, 0.0).astype(BF16), tri, preferred_element_type=F32)
        rank_rows.append(jnp.sum(jnp.where(hit, run + prefix - 1.0, 0.0), axis=0, keepdims=True))
        run = run + prefix[:, tm - 1:tm]
    w = jnp.concatenate(w_rows, axis=0)
    idx_ref[...] = jnp.concatenate(idx_rows, axis=0).astype(jnp.int32)
    wts_ref[...] = w / jnp.sum(w, axis=0, keepdims=True) * ROUTED_SCALE
    rank_ref[...] = jnp.concatenate(rank_rows, axis=0).astype(jnp.int32)
    run_full = jnp.broadcast_to(run, run_ref.shape)
    run_ref[...] = run_full
    cnt_ref[...] = run_full


def _route_call(x, g, mod, wr_t, br_col, w1s, w3s, w2s, seq):
    t, d = x.shape
    tm = TM_ROUTE
    tiles_per_seq = seq // tm
    tri = (jnp.arange(tm)[:, None] <= jnp.arange(tm)[None, :]).astype(BF16)
    return pl.pallas_call(
        _route_kernel,
        grid=(t // tm,),
        in_specs=[
            pl.BlockSpec((tm, d), lambda i: (i, 0)),
            pl.BlockSpec((1, d), lambda i: (0, 0)),
            pl.BlockSpec((1, N_MOD, d), lambda i: (i // tiles_per_seq, 0, 0)),
            pl.BlockSpec(wr_t.shape, lambda i: (0, 0)),
            pl.BlockSpec(br_col.shape, lambda i: (0, 0)),
            pl.BlockSpec(w1s.shape, lambda i: (0, 0)),
            pl.BlockSpec(w3s.shape, lambda i: (0, 0)),
            pl.BlockSpec(w2s.shape, lambda i: (0, 0)),
            pl.BlockSpec((tm, tm), lambda i: (0, 0)),
        ],
        out_specs=[
            pl.BlockSpec((tm, d // 2), lambda i: (i, 0)),
            pl.BlockSpec((tm, d), lambda i: (i, 0)),
            pl.BlockSpec((TOP_K, tm), lambda i: (0, i)),
            pl.BlockSpec((TOP_K, tm), lambda i: (0, i)),
            pl.BlockSpec((TOP_K, tm), lambda i: (0, i)),
            pl.BlockSpec((N_EXPERTS, LANES), lambda i: (0, 0)),
        ],
        out_shape=[
            jax.ShapeDtypeStruct((t, d // 2), jnp.uint32),
            jax.ShapeDtypeStruct((t, d), F32),
            jax.ShapeDtypeStruct((TOP_K, t), jnp.int32),
            jax.ShapeDtypeStruct((TOP_K, t), F32),
            jax.ShapeDtypeStruct((TOP_K, t), jnp.int32),
            jax.ShapeDtypeStruct((N_EXPERTS, LANES), F32),
        ],
        scratch_shapes=[pltpu.VMEM((N_EXPERTS, LANES), F32)],
        compiler_params=_cparams(("arbitrary",)),
        name="moe_route",
    )(x, g, mod, wr_t, br_col, w1s, w3s, w2s, tri)


def _row_plan(idx_t, rank_t, cnt, n_tok):
    tm = TM_EXP
    n_tiles = TOP_K * n_tok // tm + N_EXPERTS
    counts = cnt[:, 0].astype(jnp.int32)
    padded = (counts + tm - 1) // tm * tm
    pend = jnp.cumsum(padded)
    pstart = pend - padded
    tile_start = jnp.arange(n_tiles, dtype=jnp.int32) * tm
    tile_e = jnp.minimum(jnp.sum((pend[None, :] <= tile_start[:, None]).astype(jnp.int32), axis=1),
                         N_EXPERTS - 1).astype(jnp.int32)
    nact = (pend[-1] // tm).astype(jnp.int32).reshape(1)
    eids = jnp.arange(N_EXPERTS, dtype=jnp.int32)
    nonempty = counts > 0
    after = eids[None, :] > eids[:, None]
    ordinal = jnp.sum((~after & nonempty[None, :]).astype(jnp.int32), axis=1) - 1
    next_e = jnp.min(jnp.where(after & nonempty[None, :], eids[None, :], N_EXPERTS), axis=1)
    tile_hot = tile_e[:, None] == eids[None, :]
    lookup = lambda table: jnp.sum(jnp.where(tile_hot, table[None, :], 0), axis=1)
    tile_off = tile_start - lookup(pstart)
    is_first = (tile_off == 0).astype(jnp.int32)
    n_half = 1 + (lookup(counts) - tile_off > tm // 2).astype(jnp.int32)
    tile_info = jnp.stack([tile_e, is_first, lookup(ordinal) % 2, lookup(next_e), n_half], axis=0)
    zero_start = jnp.maximum(pend - tm, 0).astype(jnp.int32)
    experts = jnp.arange(N_EXPERTS, dtype=jnp.int32)[:, None, None]
    dest = rank_t + jnp.sum(jnp.where(idx_t[None] == experts, pstart[:, None, None], 0), axis=0)
    nt = n_tok // TM_TOK
    dest_tiles = dest.reshape(TOP_K, nt, TM_TOK).transpose(1, 0, 2).reshape(nt, 1, TOP_K * TM_TOK)
    return tile_info.astype(jnp.int32), nact, zero_start, dest_tiles.astype(jnp.int32), n_tiles * tm


def _dispatch_kernel(zs_ref, dest_ref, h_ref, xs_hbm, zbuf, zsem, sem):
    i = pl.program_id(0)
    tm = TM_TOK
    tme = zbuf.shape[0]

    def row_copy(j, dst_row):
        return pltpu.make_async_copy(h_ref.at[pl.ds(j, 1)], xs_hbm.at[pl.ds(dst_row, 1)], sem)

    @pl.when(i == 0)
    def _():
        zbuf[...] = jnp.zeros_like(zbuf)

        def zero_copy(e):
            start = pl.multiple_of(zs_ref[e], tme)
            return pltpu.make_async_copy(zbuf, xs_hbm.at[pl.ds(start, tme)], zsem)
        for e in range(N_EXPERTS):
            zero_copy(e).start()
        for e in range(N_EXPERTS):
            zero_copy(e).wait()

    for j in range(tm):
        for k in range(TOP_K):
            row_copy(j, dest_ref[0, 0, k * tm + j]).start(priority=k % 2)

    for j in range(tm):
        for _ in range(TOP_K):
            row_copy(j, 0).wait()


def _dispatch_call(zero_start, dest_tiles, hp, n_rows):
    t, dp = hp.shape
    nt = t // TM_TOK
    grid_spec = pltpu.PrefetchScalarGridSpec(
        num_scalar_prefetch=1,
        grid=(nt,),
        in_specs=[
            pl.BlockSpec((1, 1, TOP_K * TM_TOK), lambda i, zs: (i, 0, 0), memory_space=pltpu.SMEM),
            pl.BlockSpec((TM_TOK, dp), lambda i, zs: (i, 0)),
        ],
        out_specs=pl.BlockSpec(memory_space=pl.ANY),
        scratch_shapes=[
            pltpu.VMEM((TM_EXP, dp), hp.dtype),
            pltpu.SemaphoreType.DMA(()),
            pltpu.SemaphoreType.DMA(()),
        ],
    )
    return pl.pallas_call(
        _dispatch_kernel,
        grid_spec=grid_spec,
        out_shape=jax.ShapeDtypeStruct((n_rows, dp), hp.dtype),
        compiler_params=_cparams(("arbitrary",)),
        name="moe_dispatch",
    )(zero_start, dest_tiles, hp)


def _expert_kernel(ti_ref, nact_ref, x_ref, w1_hbm, w3_hbm, w2_hbm, y_ref,
                   w1f, w3f, w2f, w1b, w3b, w2b, wsem, *, layer):
    i = pl.program_id(0)

    def weight_copies(e, s):
        return (pltpu.make_async_copy(w1_hbm.at[layer, e], w1f.at[s], wsem.at[s]),
                pltpu.make_async_copy(w3_hbm.at[layer, e], w3f.at[s], wsem.at[s]),
                pltpu.make_async_copy(w2_hbm.at[layer, e], w2f.at[s], wsem.at[s]))

    @pl.when(i < nact_ref[0])
    def _():
        @pl.when(ti_ref[1, i] == 1)
        def _():
            e = ti_ref[0, i]
            nxt = ti_ref[3, i]
            for s in range(2):
                @pl.when(ti_ref[2, i] == s)
                def _():
                    @pl.when(i == 0)
                    def _():
                        for c in weight_copies(e, s):
                            c.start()
                    for c in weight_copies(e, s):
                        c.wait()

                    @pl.when(nxt < N_EXPERTS)
                    def _():
                        for c in weight_copies(nxt, 1 - s):
                            c.start()
                    w1b[...] = w1f[s].astype(BF16)
                    w3b[...] = w3f[s].astype(BF16)
                    w2b[...] = w2f[s].astype(BF16)

        half = x_ref.shape[1]
        sub = x_ref.shape[0] // 2

        def swiglu_rows(r0):
            rows = slice(r0, r0 + sub)
            x_lo, x_hi = _unpack_pair(x_ref[rows, :])
            x_lo = x_lo.astype(BF16)
            x_hi = x_hi.astype(BF16)
            a = (jnp.dot(x_lo, w1b[:half, :], preferred_element_type=F32)
                 + jnp.dot(x_hi, w1b[half:, :], preferred_element_type=F32))
            b = (jnp.dot(x_lo, w3b[:half, :], preferred_element_type=F32)
                 + jnp.dot(x_hi, w3b[half:, :], preferred_element_type=F32))
            g = (_silu(a) * b).astype(BF16)
            y_ref[rows, :] = _pack_pair(jnp.dot(g, w2b[:, :half], preferred_element_type=F32),
                                        jnp.dot(g, w2b[:, half:], preferred_element_type=F32))

        @pl.when(ti_ref[4, i] == 2)
        def _():
            swiglu_rows(0)
            swiglu_rows(sub)

        @pl.when(ti_ref[4, i] == 1)
        def _():
            swiglu_rows(0)
            y_ref[sub:, :] = jnp.zeros((sub, half), y_ref.dtype)


def _expert_call(layer, tile_info, nact, xs, w1_e, w3_e, w2_e):
    tm = TM_EXP
    n_rows, dp = xs.shape
    d = 2 * dp
    n_tiles = n_rows // tm
    de = w1_e.shape[-1]
    row_map = lambda i, ti, na: (jnp.minimum(i, na[0] - 1), 0)
    grid_spec = pltpu.PrefetchScalarGridSpec(
        num_scalar_prefetch=2,
        grid=(n_tiles,),
        in_specs=[
            pl.BlockSpec((tm, dp), row_map),
            pl.BlockSpec(memory_space=pl.ANY),
            pl.BlockSpec(memory_space=pl.ANY),
            pl.BlockSpec(memory_space=pl.ANY),
        ],
        out_specs=pl.BlockSpec((tm, dp), row_map),
        scratch_shapes=[
            pltpu.VMEM((2, d, de), F32),
            pltpu.VMEM((2, d, de), F32),
            pltpu.VMEM((2, de, d), F32),
            pltpu.VMEM((d, de), BF16),
            pltpu.VMEM((d, de), BF16),
            pltpu.VMEM((de, d), BF16),
            pltpu.SemaphoreType.DMA((2,)),
        ],
    )
    return pl.pallas_call(
        functools.partial(_expert_kernel, layer=layer),
        grid_spec=grid_spec,
        out_shape=jax.ShapeDtypeStruct((n_rows, dp), xs.dtype),
        compiler_params=_cparams(("arbitrary",)),
        name="moe_experts",
    )(tile_info, nact, xs, w1_e, w3_e, w2_e)


def _combine_kernel(dnext_ref, dfirst_ref, y_hbm, w_ref, xp_ref, mod_ref, gf_ref, out_ref,
                    ybuf0, ybuf1, sem, *, final):
    i = pl.program_id(0)
    n = pl.num_programs(0)
    tm = TM_TOK
    slot = i % 2
    ybufs = (ybuf0, ybuf1)

    def row_copy(src_row, k, j, s):
        return pltpu.make_async_copy(y_hbm.at[pl.ds(src_row, 1)], ybufs[s].at[k, pl.ds(j, 1)],
                                     sem.at[s])

    def start_rows_rolled(d_ref, s):
        def body(j, carry):
            for k in range(TOP_K):
                row_copy(d_ref[0, 0, k * tm + j], k, j, s).start(priority=k % 2)
            return carry
        lax.fori_loop(0, tm, body, 0, unroll=DMA_UNROLL)

    def start_rows(d_ref, s):
        for j in range(tm):
            for k in range(TOP_K):
                row_copy(d_ref[0, 0, k * tm + j], k, j, s).start(priority=k % 2)

    def wait_rows(s):
        for j in range(tm):
            for k in range(TOP_K):
                row_copy(0, k, j, s).wait()

    @pl.when(i == 0)
    def _():
        start_rows_rolled(dfirst_ref, 0)

    def reduce_rows(s):
        ybuf = ybufs[s]
        w = w_ref[...]
        half = ybuf.shape[2]
        acc_lo, acc_hi = _unpack_pair(ybuf[0])
        acc_lo = acc_lo * w[:, 0:1]
        acc_hi = acc_hi * w[:, 0:1]
        for k in range(1, TOP_K):
            y_lo, y_hi = _unpack_pair(ybuf[k])
            acc_lo = acc_lo + y_lo * w[:, k:k + 1]
            acc_hi = acc_hi + y_hi * w[:, k:k + 1]
        gate = mod_ref[0][5:6]
        x_lo = xp_ref[:, :half] + gate[:, :half] * acc_lo
        x_hi = xp_ref[:, half:] + gate[:, half:] * acc_hi
        if final:
            ms = (jnp.sum(x_lo * x_lo, axis=-1, keepdims=True)
                  + jnp.sum(x_hi * x_hi, axis=-1, keepdims=True)) / (2 * half)
            r = lax.rsqrt(ms + EPS)
            x_lo = x_lo * r * gf_ref[:, :half]
            x_hi = x_hi * r * gf_ref[:, half:]
        out_ref[:, :half] = x_lo
        out_ref[:, half:] = x_hi

    for s in range(2):
        @pl.when((slot == s) & (i + 1 < n))
        def _():
            wait_rows(s)
            start_rows(dnext_ref, 1 - s)
            reduce_rows(s)

        @pl.when((slot == s) & (i + 1 == n))
        def _():
            wait_rows(s)
            reduce_rows(s)


def _combine_call(dest_tiles, ys, wts_col, xp, mod, g_final, seq, final):
    t, d = xp.shape
    tm = TM_TOK
    nt = t // tm
    tiles_per_seq = seq // tm
    return pl.pallas_call(
        functools.partial(_combine_kernel, final=final),
        grid=(nt,),
        in_specs=[
            pl.BlockSpec((1, 1, TOP_K * tm), lambda i: (jnp.minimum(i + 1, nt - 1), 0, 0),
                         memory_space=pltpu.SMEM),
            pl.BlockSpec((1, 1, TOP_K * tm), lambda i: (0, 0, 0), memory_space=pltpu.SMEM),
            pl.BlockSpec(memory_space=pl.ANY),
            pl.BlockSpec((tm, TOP_K), lambda i: (i, 0)),
            pl.BlockSpec((tm, d), lambda i: (i, 0)),
            pl.BlockSpec((1, N_MOD, d), lambda i: (i // tiles_per_seq, 0, 0)),
            pl.BlockSpec((1, d), lambda i: (0, 0)),
        ],
        out_specs=pl.BlockSpec((tm, d), lambda i: (i, 0)),
        out_shape=jax.ShapeDtypeStruct((t, d), F32),
        scratch_shapes=[
            pltpu.VMEM((TOP_K, tm, ys.shape[1]), ys.dtype),
            pltpu.VMEM((TOP_K, tm, ys.shape[1]), ys.dtype),
            pltpu.SemaphoreType.DMA((2,)),
        ],
        compiler_params=_cparams(("arbitrary",)),
        name="moe_combine",
    )(dest_tiles, dest_tiles, ys, wts_col, xp, mod, g_final)


def _reorder_w_in(w):
    cq, ckv, kpe, gb, gc, u, la, lc = jnp.split(
        w, [512, 768, 832, 1856, 2880, 3904, 5952], axis=1)
    pad = jnp.zeros((w.shape[0], N_PROJ - 8000), w.dtype)
    return jnp.concatenate([la, lc, gb, gc, u, cq, ckv, kpe, pad], axis=1).astype(BF16)


def kernel(x, c, positions, w_ada, b_ada, g_mix, w_in, g_q, w_uq, g_kv, w_ukv, w_o_attn, conv_w,
           w_conv_out, w_out, g_ffn, w_router, b_router, w1_e, w3_e, w2_e, w1_s, w3_s, w2_s, g_final):
    batch, seq, d = x.shape
    depth = w_ada.shape[0]
    t = batch * seq
    xf = x.reshape(t, d)

    c8 = jnp.zeros((8, d), F32).at[:batch].set(c)
    mod_all = _ada_call(c8, w_ada, b_ada.reshape(depth, 1, N_MOD * d))

    inv = 1.0 / (ROPE_THETA ** (jnp.arange(0, QK_ROPE, 2, dtype=F32) / QK_ROPE))
    inv128 = jnp.concatenate([inv, inv, jnp.zeros((LANES - QK_ROPE,), F32)]).reshape(1, LANES)
    cos_t, sin_t = _rope_call(positions.reshape(t, 1), inv128)

    for l in range(depth):
        mod = mod_all[l, :batch].reshape(batch, N_MOD, d)
        wq = jnp.pad(w_uq[l], ((0, 0), (0, 0), (0, HEAD_PAD - QK_HEAD))).reshape(
            Q_LORA, N_HEADS * HEAD_PAD).astype(BF16)
        wkv = w_ukv[l].reshape(KV_LORA, N_HEADS * (QK_NOPE + V_HEAD)).astype(BF16)

        proj = _in_call(xf, g_mix[l].reshape(1, d), mod, _reorder_w_in(w_in[l]), seq)
        q, k, v = _qkv_call(proj, cos_t, sin_t, g_q[l].reshape(1, Q_LORA),
                            g_kv[l].reshape(1, KV_LORA), wq, wkv)
        o = _attn_call(q, k, v, batch, seq)
        xf = _merge_call(o, proj, xf, mod, conv_w[l].reshape(3, CONV_WIDTH),
                         w_o_attn[l].astype(BF16), w_conv_out[l].astype(BF16),
                         w_out[l].astype(BF16), seq)

        h, xp, idx_t, wts_t, rank_t, cnt = _route_call(
            xf, g_ffn[l].reshape(1, d), mod, w_router[l].T, b_router[l].reshape(N_EXPERTS, 1),
            w1_s[l].astype(BF16), w3_s[l].astype(BF16), w2_s[l].astype(BF16), seq)
        tile_info, nact, zero_start, dest_tiles, n_rows = _row_plan(idx_t, rank_t, cnt, t)
        xs = _dispatch_call(zero_start, dest_tiles, h, n_rows)
        ys = _expert_call(l, tile_info, nact, xs, w1_e, w3_e, w2_e)
        xf = _combine_call(dest_tiles, ys, wts_t.T, xp, mod, g_final.reshape(1, d), seq,
                           final=(l == depth - 1))

    return xf.reshape(batch, seq, d)
```

```python
import functools

import jax
import jax.numpy as jnp
from jax import lax
from jax.experimental import pallas as pl
from jax.experimental.pallas import tpu as pltpu

F32 = jnp.float32
BF16 = jnp.bfloat16

N_HEADS = 16
Q_LORA = 512
KV_LORA = 256
QK_NOPE = 128
QK_ROPE = 64
V_HEAD = 128
QK_HEAD = QK_NOPE + QK_ROPE
ROPE_THETA = 10000.0
CONV_WIDTH = 1024
N_EXPERTS = 64
TOP_K = 8
N_GROUPS = 8
TOPK_GROUPS = 4
EXPERTS_PER_GROUP = N_EXPERTS // N_GROUPS
ROUTED_SCALE = 2.5
N_MOD = 6
EPS = 1e-6
LOG2_E = 1.4426950408889634

LANES = 128
BF16_SUBLANES = 16
HEAD_PAD = 2 * LANES

COL_LA = 0
COL_LC = 2048
COL_GB = 4096
COL_GC = 5120
COL_U = 6144
COL_CQ = 7168
COL_CKV = 7680
COL_KPE = 7936
N_PROJ = 8064

VMEM_LIMIT = 56 * 1024 * 1024

TM_IN = 1024
TN_IN = 1152
TM_QKV = 512
TQ_ATTN = 1024
HEADS_PER_STEP = 1
TK_ATTN = 512
TM_MERGE = 256
TM_ROUTE = 512
TM_EXP = 512
ZERO_ROWS = TM_EXP // 2
TM_TOK = 128
TN_ADA = 2048
DMA_UNROLL = 8


def _cparams(sem):
    return pltpu.CompilerParams(dimension_semantics=sem, vmem_limit_bytes=VMEM_LIMIT)


def _rms(x):
    return x * lax.rsqrt(jnp.mean(x * x, axis=-1, keepdims=True) + EPS)


def _silu(x):
    return x * jax.nn.sigmoid(x)


HI_HALF = 0xFFFF0000


def _pack_pair(lo, hi):
    lo_bits = lax.bitcast_convert_type(lo.astype(BF16).astype(F32), jnp.uint32) >> 16
    hi_bits = lax.bitcast_convert_type(hi.astype(BF16).astype(F32), jnp.uint32) & jnp.uint32(HI_HALF)
    return hi_bits | lo_bits


def _unpack_pair(u):
    lo = lax.bitcast_convert_type(u << 16, F32)
    hi = lax.bitcast_convert_type(u & jnp.uint32(HI_HALF), F32)
    return lo, hi


def _ada_kernel(c_ref, w_ref, b_ref, o_ref):
    ca = _silu(c_ref[...]).astype(BF16)
    o_ref[0] = jnp.dot(ca, w_ref[0].astype(BF16), preferred_element_type=F32) + b_ref[0]


def _ada_call(c8, w_ada, b_ada3):
    depth, d, n = w_ada.shape
    return pl.pallas_call(
        _ada_kernel,
        grid=(depth, n // TN_ADA),
        in_specs=[
            pl.BlockSpec((8, d), lambda l, j: (0, 0)),
            pl.BlockSpec((1, d, TN_ADA), lambda l, j: (l, 0, j)),
            pl.BlockSpec((1, 1, TN_ADA), lambda l, j: (l, 0, j)),
        ],
        out_specs=pl.BlockSpec((1, 8, TN_ADA), lambda l, j: (l, 0, j)),
        out_shape=jax.ShapeDtypeStruct((depth, 8, n), F32),
        compiler_params=_cparams(("parallel", "parallel")),
        name="ada_mod",
    )(c8, w_ada, b_ada3)


def _rope_kernel(pos_ref, inv_ref, cos_ref, sin_ref):
    ang = pos_ref[...].astype(F32) * inv_ref[...]
    lane = lax.broadcasted_iota(jnp.int32, ang.shape, 1)
    half = QK_ROPE // 2
    cos_ref[...] = jnp.where(lane < QK_ROPE, jnp.cos(ang), 0.0)
    s = jnp.sin(ang)
    sin_ref[...] = jnp.where(lane < half, -s, jnp.where(lane < QK_ROPE, s, 0.0))


def _rope_call(pos_col, inv128):
    t = pos_col.shape[0]
    tm = 1024
    return pl.pallas_call(
        _rope_kernel,
        grid=(t // tm,),
        in_specs=[pl.BlockSpec((tm, 1), lambda i: (i, 0)),
                  pl.BlockSpec((1, LANES), lambda i: (0, 0))],
        out_specs=[pl.BlockSpec((tm, LANES), lambda i: (i, 0)),
                   pl.BlockSpec((tm, LANES), lambda i: (i, 0))],
        out_shape=[jax.ShapeDtypeStruct((t, LANES), F32)] * 2,
        compiler_params=_cparams(("parallel",)),
        name="rope_tables",
    )(pos_col, inv128)


def _in_kernel(x_ref, g_ref, mod_ref, w_ref, o_ref, h_ref):
    @pl.when(pl.program_id(1) == 0)
    def _():
        m = mod_ref[0]
        h = _rms(x_ref[...]) * g_ref[...]
        h_ref[...] = (h * (1.0 + m[1:2]) + m[0:1]).astype(BF16)

    o_ref[...] = jnp.dot(h_ref[...], w_ref[...], preferred_element_type=F32).astype(o_ref.dtype)


def _in_call(x, g, mod, w_in_r, seq):
    t, d = x.shape
    tiles_per_seq = seq // TM_IN
    return pl.pallas_call(
        _in_kernel,
        grid=(t // TM_IN, N_PROJ // TN_IN),
        in_specs=[
            pl.BlockSpec((TM_IN, d), lambda i, j: (i, 0)),
            pl.BlockSpec((1, d), lambda i, j: (0, 0)),
            pl.BlockSpec((1, N_MOD, d), lambda i, j: (i // tiles_per_seq, 0, 0)),
            pl.BlockSpec((d, TN_IN), lambda i, j: (0, j)),
        ],
        out_specs=pl.BlockSpec((TM_IN, TN_IN), lambda i, j: (i, j)),
        out_shape=jax.ShapeDtypeStruct((t, N_PROJ), BF16),
        scratch_shapes=[pltpu.VMEM((TM_IN, d), BF16)],
        compiler_params=_cparams(("parallel", "arbitrary")),
        name="in_proj",
    )(x, g, mod, w_in_r)


def _rope128(t, cos, sin_signed):
    lane = lax.broadcasted_iota(jnp.int32, t.shape, 1)
    half = QK_ROPE // 2
    partner = jnp.where(lane < half, pltpu.roll(t, LANES - half, 1), pltpu.roll(t, half, 1))
    return t * cos + partner * sin_signed


def _qkv_kernel(cq_ref, ckv_ref, kpe_ref, cos_ref, sin_ref, gq_ref, gkv_ref,
                wq_ref, wkv_ref, q_ref, k_ref, v_ref):
    cos = cos_ref[...]
    sin = sin_ref[...]
    qn = (_rms(cq_ref[...].astype(F32)) * gq_ref[...]).astype(BF16)
    q = jnp.dot(qn, wq_ref[...], preferred_element_type=F32) * (QK_HEAD ** -0.5 * LOG2_E)
    cn = (_rms(ckv_ref[...].astype(F32)) * gkv_ref[...]).astype(BF16)
    kv = jnp.dot(cn, wkv_ref[...], preferred_element_type=F32).astype(BF16)
    kr = _rope128(kpe_ref[...].astype(F32), cos, sin).astype(BF16)
    lane = lax.broadcasted_iota(jnp.int32, kr.shape, 1)
    ones_col = jnp.where(lane == 0, 1.0, 0.0).astype(BF16)
    for h in range(N_HEADS):
        base = h * HEAD_PAD
        v_ref[:, base:base + V_HEAD] = kv[:, base + QK_NOPE:base + HEAD_PAD]
        v_ref[:, base + V_HEAD:base + HEAD_PAD] = ones_col
        q_ref[:, base:base + LANES] = q[:, base:base + LANES].astype(BF16)
        q_ref[:, base + LANES:base + HEAD_PAD] = _rope128(
            q[:, base + LANES:base + HEAD_PAD], cos, sin).astype(BF16)
        k_ref[:, base:base + LANES] = kv[:, base:base + QK_NOPE]
        k_ref[:, base + LANES:base + HEAD_PAD] = kr


def _qkv_call(proj, cos, sin, gq, gkv, wq, wkv):
    t = proj.shape[0]
    tm = TM_QKV
    return pl.pallas_call(
        _qkv_kernel,
        grid=(t // tm,),
        in_specs=[
            pl.BlockSpec((tm, Q_LORA), lambda i: (i, COL_CQ // Q_LORA)),
            pl.BlockSpec((tm, KV_LORA), lambda i: (i, COL_CKV // KV_LORA)),
            pl.BlockSpec((tm, LANES), lambda i: (i, COL_KPE // LANES)),
            pl.BlockSpec((tm, LANES), lambda i: (i, 0)),
            pl.BlockSpec((tm, LANES), lambda i: (i, 0)),
            pl.BlockSpec((1, Q_LORA), lambda i: (0, 0)),
            pl.BlockSpec((1, KV_LORA), lambda i: (0, 0)),
            pl.BlockSpec(wq.shape, lambda i: (0, 0)),
            pl.BlockSpec(wkv.shape, lambda i: (0, 0)),
        ],
        out_specs=[
            pl.BlockSpec((tm, N_HEADS * HEAD_PAD), lambda i: (i, 0)),
            pl.BlockSpec((tm, N_HEADS * HEAD_PAD), lambda i: (i, 0)),
            pl.BlockSpec((tm, N_HEADS * HEAD_PAD), lambda i: (i, 0)),
        ],
        out_shape=[
            jax.ShapeDtypeStruct((t, N_HEADS * HEAD_PAD), BF16),
            jax.ShapeDtypeStruct((t, N_HEADS * HEAD_PAD), BF16),
            jax.ShapeDtypeStruct((t, N_HEADS * HEAD_PAD), BF16),
        ],
        compiler_params=_cparams(("parallel",)),
        name="qkv_build",
    )(proj, proj, proj, cos, sin, gq, gkv, wq, wkv)


def _attn_kernel(q_ref, k_ref, v_ref, o_ref):
    tq = q_ref.shape[0]
    n_chunks = k_ref.shape[0] // TK_ATTN
    for hh in range(HEADS_PER_STEP):
        qk = slice(hh * HEAD_PAD, (hh + 1) * HEAD_PAD)
        vo = slice(hh * V_HEAD, (hh + 1) * V_HEAD)
        q = q_ref[:, qk]
        m = jnp.full((tq, 1), -jnp.inf, F32)
        acc = jnp.zeros((tq, HEAD_PAD), F32)
        for j in range(n_chunks):
            rows = slice(j * TK_ATTN, (j + 1) * TK_ATTN)
            s = lax.dot_general(q, k_ref[rows, qk], (((1,), (1,)), ((), ())),
                                preferred_element_type=F32)
            m_new = jnp.maximum(m, jnp.max(s, axis=-1, keepdims=True))
            alpha = jnp.exp2(m - m_new)
            p = jnp.exp2(s - m_new)
            acc = alpha * acc + jnp.dot(p.astype(BF16), v_ref[rows, qk],
                                        preferred_element_type=F32)
            m = m_new
        o_ref[:, vo] = (acc[:, :V_HEAD] / acc[:, V_HEAD:V_HEAD + 1]).astype(o_ref.dtype)


def _attn_call(q, k, v, batch, seq):
    t = q.shape[0]
    nq = seq // TQ_ATTN
    hs = HEADS_PER_STEP
    return pl.pallas_call(
        _attn_kernel,
        grid=(batch, N_HEADS // hs, nq),
        in_specs=[
            pl.BlockSpec((TQ_ATTN, hs * HEAD_PAD), lambda b, h, i: (b * nq + i, h)),
            pl.BlockSpec((seq, hs * HEAD_PAD), lambda b, h, i: (b, h)),
            pl.BlockSpec((seq, hs * HEAD_PAD), lambda b, h, i: (b, h)),
        ],
        out_specs=pl.BlockSpec((TQ_ATTN, hs * V_HEAD), lambda b, h, i: (b * nq + i, h)),
        out_shape=jax.ShapeDtypeStruct((t, N_HEADS * V_HEAD), BF16),
        compiler_params=_cparams(("parallel", "parallel", "arbitrary")),
        name="mla_attention",
    )(q, k, v)


def _merge_kernel(o_ref, la_ref, lc_ref, gb_ref, gc_ref, u_ref, gcp_ref, up_ref, gcn_ref, un_ref,
                  x_ref, mod_ref, cw_ref, wo_ref, wco_ref, wout_ref, out_ref, *, tiles_per_seq):
    i = pl.program_id(0)
    tm = x_ref.shape[0]
    pos = i % tiles_per_seq
    has_prev = (pos != 0).astype(F32)
    has_next = (pos != tiles_per_seq - 1).astype(F32)
    z = gc_ref[...].astype(F32) * u_ref[...].astype(F32)
    last = BF16_SUBLANES - 1
    z_before = gcp_ref[last:last + 1, :].astype(F32) * up_ref[last:last + 1, :].astype(F32) * has_prev
    z_after = gcn_ref[0:1, :].astype(F32) * un_ref[0:1, :].astype(F32) * has_next
    row = lax.broadcasted_iota(jnp.int32, z.shape, 0)
    z_prev = jnp.where(row == 0, z_before, pltpu.roll(z, 1, 0))
    z_next = jnp.where(row == tm - 1, z_after, pltpu.roll(z, tm - 1, 0))
    cw = cw_ref[...]
    conv = cw[0:1] * z_prev + cw[1:2] * z + cw[2:3] * z_next
    y_conv = jnp.dot((gb_ref[...].astype(F32) * conv).astype(BF16), wco_ref[...],
                     preferred_element_type=F32)
    y_attn = jnp.dot(o_ref[...], wo_ref[...], preferred_element_type=F32)
    mix = (jax.nn.sigmoid(la_ref[...].astype(F32)) * y_attn
           + jax.nn.sigmoid(lc_ref[...].astype(F32)) * y_conv)
    y = jnp.dot(mix.astype(BF16), wout_ref[...], preferred_element_type=F32)
    out_ref[...] = x_ref[...] + mod_ref[0][2:3] * y


def _merge_call(o, proj, x, mod, conv_w, w_o, w_co, w_out, seq):
    t, d = x.shape
    tm = TM_MERGE
    tiles_per_seq = seq // tm
    hb = tm // BF16_SUBLANES
    n_halo = t // BF16_SUBLANES
    cw_blocks = CONV_WIDTH
    const = dict(pipeline_mode=pl.Buffered(1))
    prev_map = lambda c: (lambda i: (jnp.maximum(i * hb - 1, 0), c))
    next_map = lambda c: (lambda i: (jnp.minimum((i + 1) * hb, n_halo - 1), c))
    return pl.pallas_call(
        functools.partial(_merge_kernel, tiles_per_seq=tiles_per_seq),
        grid=(t // tm,),
        in_specs=[
            pl.BlockSpec((tm, d), lambda i: (i, 0)),
            pl.BlockSpec((tm, d), lambda i: (i, COL_LA // d)),
            pl.BlockSpec((tm, d), lambda i: (i, COL_LC // d)),
            pl.BlockSpec((tm, cw_blocks), lambda i: (i, COL_GB // cw_blocks)),
            pl.BlockSpec((tm, cw_blocks), lambda i: (i, COL_GC // cw_blocks)),
            pl.BlockSpec((tm, cw_blocks), lambda i: (i, COL_U // cw_blocks)),
            pl.BlockSpec((BF16_SUBLANES, cw_blocks), prev_map(COL_GC // cw_blocks)),
            pl.BlockSpec((BF16_SUBLANES, cw_blocks), prev_map(COL_U // cw_blocks)),
            pl.BlockSpec((BF16_SUBLANES, cw_blocks), next_map(COL_GC // cw_blocks)),
            pl.BlockSpec((BF16_SUBLANES, cw_blocks), next_map(COL_U // cw_blocks)),
            pl.BlockSpec((tm, d), lambda i: (i, 0)),
            pl.BlockSpec((1, N_MOD, d), lambda i: (i // tiles_per_seq, 0, 0)),
            pl.BlockSpec(conv_w.shape, lambda i: (0, 0)),
            pl.BlockSpec(w_o.shape, lambda i: (0, 0), **const),
            pl.BlockSpec(w_co.shape, lambda i: (0, 0), **const),
            pl.BlockSpec(w_out.shape, lambda i: (0, 0), **const),
        ],
        out_specs=pl.BlockSpec((tm, d), lambda i: (i, 0)),
        out_shape=jax.ShapeDtypeStruct((t, d), F32),
        compiler_params=_cparams(("parallel",)),
        name="conv_merge_out",
    )(o, proj, proj, proj, proj, proj, proj, proj, proj, proj, x, mod, conv_w, w_o, w_co, w_out)


def _first_max(vals, iota, sentinel):
    m = jnp.max(vals, axis=0, keepdims=True)
    first = jnp.min(jnp.where(vals == m, iota, sentinel), axis=0, keepdims=True)
    return m, first, iota == first


def _route_kernel(x_ref, g_ref, mod_ref, wr_ref, br_ref, tri_ref,
                  h_ref, idx_ref, wts_ref, rank_ref, cnt_ref, run_ref):
    @pl.when(pl.program_id(0) == 0)
    def _():
        run_ref[...] = jnp.zeros_like(run_ref)

    m = mod_ref[0]
    x = x_ref[...]
    h = (_rms(x) * g_ref[...]) * (1.0 + m[4:5]) + m[3:4]
    half = h.shape[1] // 2
    h_ref[...] = _pack_pair(h[:, :half], h[:, half:])

    logits = lax.dot_general(wr_ref[...], h, (((1,), (1,)), ((), ())),
                             precision=lax.Precision.HIGHEST, preferred_element_type=F32)
    scores = jax.nn.sigmoid(logits)
    biased = scores + br_ref[...]
    tm = scores.shape[1]
    neg = -jnp.inf
    g = EXPERTS_PER_GROUP
    iota_g = lax.broadcasted_iota(jnp.int32, (g, tm), 0).astype(F32)
    group_rows = []
    for gi in range(N_GROUPS):
        blk = biased[gi * g:(gi + 1) * g, :]
        m1, _, hit = _first_max(blk, iota_g, g)
        m2 = jnp.max(jnp.where(hit, neg, blk), axis=0, keepdims=True)
        group_rows.append(m1 + m2)
    gs = jnp.concatenate(group_rows, axis=0)
    iota_n = lax.broadcasted_iota(jnp.int32, (N_GROUPS, tm), 0).astype(F32)
    chosen = jnp.zeros((N_GROUPS, tm), F32)
    for _ in range(TOPK_GROUPS):
        _, _, hit = _first_max(gs, iota_n, N_GROUPS)
        chosen = jnp.where(hit, 1.0, chosen)
        gs = jnp.where(hit, neg, gs)
    allowed = jnp.concatenate(
        [jnp.broadcast_to(chosen[gi:gi + 1, :], (g, tm)) for gi in range(N_GROUPS)], axis=0)
    cand = jnp.where(allowed > 0.0, biased, neg)
    iota_e = lax.broadcasted_iota(jnp.int32, (N_EXPERTS, tm), 0).astype(F32)

    run = run_ref[:, 0:1]
    tri = tri_ref[...]
    idx_rows, w_rows, rank_rows = [], [], []
    for _ in range(TOP_K):
        _, first, hit = _first_max(cand, iota_e, N_EXPERTS)
        idx_rows.append(first)
        w_rows.append(jnp.sum(jnp.where(hit, scores, 0.0), axis=0, keepdims=True))
        cand = jnp.where(hit, neg, cand)
        prefix = jnp.dot(jnp.where(hit, 1.0, 0.0).astype(BF16), tri, preferred_element_type=F32)
        rank_rows.append(jnp.sum(jnp.where(hit, run + prefix - 1.0, 0.0), axis=0, keepdims=True))
        run = run + prefix[:, tm - 1:tm]
    w = jnp.concatenate(w_rows, axis=0)
    idx_ref[...] = jnp.concatenate(idx_rows, axis=0).astype(jnp.int32)
    wts_ref[...] = w / jnp.sum(w, axis=0, keepdims=True) * ROUTED_SCALE
    rank_ref[...] = jnp.concatenate(rank_rows, axis=0).astype(jnp.int32)
    run_full = jnp.broadcast_to(run, run_ref.shape)
    run_ref[...] = run_full
    cnt_ref[...] = run_full


def _route_call(x, g, mod, wr_t, br_col, seq):
    t, d = x.shape
    tm = TM_ROUTE
    tiles_per_seq = seq // tm
    tri = (jnp.arange(tm)[:, None] <= jnp.arange(tm)[None, :]).astype(BF16)
    return pl.pallas_call(
        _route_kernel,
        grid=(t // tm,),
        in_specs=[
            pl.BlockSpec((tm, d), lambda i: (i, 0)),
            pl.BlockSpec((1, d), lambda i: (0, 0)),
            pl.BlockSpec((1, N_MOD, d), lambda i: (i // tiles_per_seq, 0, 0)),
            pl.BlockSpec(wr_t.shape, lambda i: (0, 0)),
            pl.BlockSpec(br_col.shape, lambda i: (0, 0)),
            pl.BlockSpec((tm, tm), lambda i: (0, 0)),
        ],
        out_specs=[
            pl.BlockSpec((tm, d // 2), lambda i: (i, 0)),
            pl.BlockSpec((TOP_K, tm), lambda i: (0, i)),
            pl.BlockSpec((TOP_K, tm), lambda i: (0, i)),
            pl.BlockSpec((TOP_K, tm), lambda i: (0, i)),
            pl.BlockSpec((N_EXPERTS, LANES), lambda i: (0, 0)),
        ],
        out_shape=[
            jax.ShapeDtypeStruct((t, d // 2), jnp.uint32),
            jax.ShapeDtypeStruct((TOP_K, t), jnp.int32),
            jax.ShapeDtypeStruct((TOP_K, t), F32),
            jax.ShapeDtypeStruct((TOP_K, t), jnp.int32),
            jax.ShapeDtypeStruct((N_EXPERTS, LANES), F32),
        ],
        scratch_shapes=[pltpu.VMEM((N_EXPERTS, LANES), F32)],
        compiler_params=_cparams(("arbitrary",)),
        name="moe_route",
    )(x, g, mod, wr_t, br_col, tri)


def _row_plan(idx_t, rank_t, cnt, n_tok):
    tm = TM_EXP
    n_tiles = TOP_K * n_tok // tm + N_EXPERTS
    counts = cnt[:, 0].astype(jnp.int32)
    padded = (counts + tm - 1) // tm * tm
    pend = jnp.cumsum(padded)
    pstart = pend - padded
    tile_start = jnp.arange(n_tiles, dtype=jnp.int32) * tm
    tile_e = jnp.minimum(jnp.sum((pend[None, :] <= tile_start[:, None]).astype(jnp.int32), axis=1),
                         N_EXPERTS - 1).astype(jnp.int32)
    nact = (pend[-1] // tm).astype(jnp.int32).reshape(1)
    eids = jnp.arange(N_EXPERTS, dtype=jnp.int32)
    nonempty = counts > 0
    after = eids[None, :] > eids[:, None]
    ordinal = jnp.sum((~after & nonempty[None, :]).astype(jnp.int32), axis=1) - 1
    next_e = jnp.min(jnp.where(after & nonempty[None, :], eids[None, :], N_EXPERTS), axis=1)
    tile_hot = tile_e[:, None] == eids[None, :]
    lookup = lambda table: jnp.sum(jnp.where(tile_hot, table[None, :], 0), axis=1)
    tile_off = tile_start - lookup(pstart)
    is_first = (tile_off == 0).astype(jnp.int32)
    n_half = 1 + (lookup(counts) - tile_off > tm // 2).astype(jnp.int32)
    tile_info = jnp.stack([tile_e, is_first, lookup(ordinal) % 2, lookup(next_e), n_half], axis=0)
    zero_start = (pstart + jnp.maximum(counts - 1, 0) // ZERO_ROWS * ZERO_ROWS).astype(jnp.int32)
    experts = jnp.arange(N_EXPERTS, dtype=jnp.int32)[:, None, None]
    dest = rank_t + jnp.sum(jnp.where(idx_t[None] == experts, pstart[:, None, None], 0), axis=0)
    nt = n_tok // TM_TOK
    dest_tiles = dest.reshape(TOP_K, nt, TM_TOK).transpose(1, 0, 2).reshape(nt, 1, TOP_K * TM_TOK)
    return tile_info.astype(jnp.int32), nact, zero_start, dest_tiles.astype(jnp.int32), n_tiles * tm


def _dispatch_kernel(zs_ref, dest_ref, h_ref, x_ref, mod_ref, w1_ref, w3_ref, w2_ref,
                     xs_hbm, xp_ref, zbuf, zsem, sem):
    i = pl.program_id(0)
    tm = TM_TOK
    tme = zbuf.shape[0]

    def row_copy(j, dst_row):
        return pltpu.make_async_copy(h_ref.at[pl.ds(j, 1)], xs_hbm.at[pl.ds(dst_row, 1)], sem)

    @pl.when(i == 0)
    def _():
        zbuf[...] = jnp.zeros_like(zbuf)

        def zero_copy(e):
            start = pl.multiple_of(zs_ref[e], tme)
            return pltpu.make_async_copy(zbuf, xs_hbm.at[pl.ds(start, tme)], zsem)
        for e in range(N_EXPERTS):
            zero_copy(e).start()
        for e in range(N_EXPERTS):
            zero_copy(e).wait()

    for j in range(tm):
        for k in range(TOP_K):
            row_copy(j, dest_ref[0, 0, k * tm + j]).start(priority=k % 2)

    half = h_ref.shape[1]
    h_lo, h_hi = _unpack_pair(h_ref[...])
    h_lo = h_lo.astype(BF16)
    h_hi = h_hi.astype(BF16)
    a = (jnp.dot(h_lo, w1_ref[:half, :], preferred_element_type=F32)
         + jnp.dot(h_hi, w1_ref[half:, :], preferred_element_type=F32))
    b = (jnp.dot(h_lo, w3_ref[:half, :], preferred_element_type=F32)
         + jnp.dot(h_hi, w3_ref[half:, :], preferred_element_type=F32))
    shared = jnp.dot((_silu(a) * b).astype(BF16), w2_ref[...], preferred_element_type=F32)
    xp_ref[...] = x_ref[...] + mod_ref[0][5:6] * shared

    for j in range(tm):
        for _ in range(TOP_K):
            row_copy(j, 0).wait()


def _dispatch_call(zero_start, dest_tiles, hp, x, mod, w1s, w3s, w2s, n_rows, seq):
    t, dp = hp.shape
    d = x.shape[1]
    tm = TM_TOK
    nt = t // tm
    tiles_per_seq = seq // tm
    grid_spec = pltpu.PrefetchScalarGridSpec(
        num_scalar_prefetch=1,
        grid=(nt,),
        in_specs=[
            pl.BlockSpec((1, 1, TOP_K * tm), lambda i, zs: (i, 0, 0), memory_space=pltpu.SMEM),
            pl.BlockSpec((tm, dp), lambda i, zs: (i, 0)),
            pl.BlockSpec((tm, d), lambda i, zs: (i, 0)),
            pl.BlockSpec((1, N_MOD, d), lambda i, zs: (i // tiles_per_seq, 0, 0)),
            pl.BlockSpec(w1s.shape, lambda i, zs: (0, 0)),
            pl.BlockSpec(w3s.shape, lambda i, zs: (0, 0)),
            pl.BlockSpec(w2s.shape, lambda i, zs: (0, 0)),
        ],
        out_specs=[
            pl.BlockSpec(memory_space=pl.ANY),
            pl.BlockSpec((tm, d), lambda i, zs: (i, 0)),
        ],
        scratch_shapes=[
            pltpu.VMEM((ZERO_ROWS, dp), hp.dtype),
            pltpu.SemaphoreType.DMA(()),
            pltpu.SemaphoreType.DMA(()),
        ],
    )
    return pl.pallas_call(
        _dispatch_kernel,
        grid_spec=grid_spec,
        out_shape=[jax.ShapeDtypeStruct((n_rows, dp), hp.dtype),
                   jax.ShapeDtypeStruct((t, d), F32)],
        compiler_params=_cparams(("arbitrary",)),
        name="moe_dispatch",
    )(zero_start, dest_tiles, hp, x, mod, w1s, w3s, w2s)


def _expert_kernel(ti_ref, nact_ref, x_ref, w1_hbm, w3_hbm, w2_hbm, y_ref,
                   w1f, w3f, w2f, w1b, w3b, w2b, wsem, *, layer):
    i = pl.program_id(0)

    def weight_copies(e, s):
        return (pltpu.make_async_copy(w1_hbm.at[layer, e], w1f.at[s], wsem.at[s]),
                pltpu.make_async_copy(w3_hbm.at[layer, e], w3f.at[s], wsem.at[s]),
                pltpu.make_async_copy(w2_hbm.at[layer, e], w2f.at[s], wsem.at[s]))

    @pl.when(i < nact_ref[0])
    def _():
        @pl.when(ti_ref[1, i] == 1)
        def _():
            e = ti_ref[0, i]
            nxt = ti_ref[3, i]
            for s in range(2):
                @pl.when(ti_ref[2, i] == s)
                def _():
                    @pl.when(i == 0)
                    def _():
                        for c in weight_copies(e, s):
                            c.start()
                    for c in weight_copies(e, s):
                        c.wait()

                    @pl.when(nxt < N_EXPERTS)
                    def _():
                        for c in weight_copies(nxt, 1 - s):
                            c.start()
                    w1b[...] = w1f[s].astype(BF16)
                    w3b[...] = w3f[s].astype(BF16)
                    w2b[...] = w2f[s].astype(BF16)

        half = x_ref.shape[1]
        sub = x_ref.shape[0] // 2

        def swiglu_rows(r0):
            rows = slice(r0, r0 + sub)
            x_lo, x_hi = _unpack_pair(x_ref[rows, :])
            x_lo = x_lo.astype(BF16)
            x_hi = x_hi.astype(BF16)
            a = (jnp.dot(x_lo, w1b[:half, :], preferred_element_type=F32)
                 + jnp.dot(x_hi, w1b[half:, :], preferred_element_type=F32))
            b = (jnp.dot(x_lo, w3b[:half, :], preferred_element_type=F32)
                 + jnp.dot(x_hi, w3b[half:, :], preferred_element_type=F32))
            g = (_silu(a) * b).astype(BF16)
            y_ref[rows, :] = _pack_pair(jnp.dot(g, w2b[:, :half], preferred_element_type=F32),
                                        jnp.dot(g, w2b[:, half:], preferred_element_type=F32))

        @pl.when(ti_ref[4, i] == 2)
        def _():
            swiglu_rows(0)
            swiglu_rows(sub)

        @pl.when(ti_ref[4, i] == 1)
        def _():
            swiglu_rows(0)
            y_ref[sub:, :] = jnp.zeros((sub, half), y_ref.dtype)


def _expert_call(layer, tile_info, nact, xs, w1_e, w3_e, w2_e):
    tm = TM_EXP
    n_rows, dp = xs.shape
    d = 2 * dp
    n_tiles = n_rows // tm
    de = w1_e.shape[-1]
    row_map = lambda i, ti, na: (jnp.minimum(i, na[0] - 1), 0)
    grid_spec = pltpu.PrefetchScalarGridSpec(
        num_scalar_prefetch=2,
        grid=(n_tiles,),
        in_specs=[
            pl.BlockSpec((tm, dp), row_map),
            pl.BlockSpec(memory_space=pl.ANY),
            pl.BlockSpec(memory_space=pl.ANY),
            pl.BlockSpec(memory_space=pl.ANY),
        ],
        out_specs=pl.BlockSpec((tm, dp), row_map),
        scratch_shapes=[
            pltpu.VMEM((2, d, de), F32),
            pltpu.VMEM((2, d, de), F32),
            pltpu.VMEM((2, de, d), F32),
            pltpu.VMEM((d, de), BF16),
            pltpu.VMEM((d, de), BF16),
            pltpu.VMEM((de, d), BF16),
            pltpu.SemaphoreType.DMA((2,)),
        ],
    )
    return pl.pallas_call(
        functools.partial(_expert_kernel, layer=layer),
        grid_spec=grid_spec,
        out_shape=jax.ShapeDtypeStruct((n_rows, dp), xs.dtype),
        compiler_params=_cparams(("arbitrary",)),
        name="moe_experts",
    )(tile_info, nact, xs, w1_e, w3_e, w2_e)


def _combine_kernel(dnext_ref, dfirst_ref, y_hbm, w_ref, xp_ref, mod_ref, gf_ref, out_ref,
                    ybuf0, ybuf1, sem, *, final):
    i = pl.program_id(0)
    n = pl.num_programs(0)
    tm = TM_TOK
    slot = i % 2
    ybufs = (ybuf0, ybuf1)

    def row_copy(src_row, k, j, s):
        return pltpu.make_async_copy(y_hbm.at[pl.ds(src_row, 1)], ybufs[s].at[k, pl.ds(j, 1)],
                                     sem.at[s])

    def start_rows_rolled(d_ref, s):
        def body(j, carry):
            for k in range(TOP_K):
                row_copy(d_ref[0, 0, k * tm + j], k, j, s).start(priority=k % 2)
            return carry
        lax.fori_loop(0, tm, body, 0, unroll=DMA_UNROLL)

    def start_rows(d_ref, s):
        for j in range(tm):
            for k in range(TOP_K):
                row_copy(d_ref[0, 0, k * tm + j], k, j, s).start(priority=k % 2)

    def wait_rows(s):
        for j in range(tm):
            for k in range(TOP_K):
                row_copy(0, k, j, s).wait()

    @pl.when(i == 0)
    def _():
        start_rows_rolled(dfirst_ref, 0)

    def reduce_rows(s):
        ybuf = ybufs[s]
        w = w_ref[...]
        half = ybuf.shape[2]
        acc_lo, acc_hi = _unpack_pair(ybuf[0])
        acc_lo = acc_lo * w[:, 0:1]
        acc_hi = acc_hi * w[:, 0:1]
        for k in range(1, TOP_K):
            y_lo, y_hi = _unpack_pair(ybuf[k])
            acc_lo = acc_lo + y_lo * w[:, k:k + 1]
            acc_hi = acc_hi + y_hi * w[:, k:k + 1]
        gate = mod_ref[0][5:6]
        x_lo = xp_ref[:, :half] + gate[:, :half] * acc_lo
        x_hi = xp_ref[:, half:] + gate[:, half:] * acc_hi
        if final:
            ms = (jnp.sum(x_lo * x_lo, axis=-1, keepdims=True)
                  + jnp.sum(x_hi * x_hi, axis=-1, keepdims=True)) / (2 * half)
            r = lax.rsqrt(ms + EPS)
            x_lo = x_lo * r * gf_ref[:, :half]
            x_hi = x_hi * r * gf_ref[:, half:]
        out_ref[:, :half] = x_lo
        out_ref[:, half:] = x_hi

    for s in range(2):
        @pl.when((slot == s) & (i + 1 < n))
        def _():
            wait_rows(s)
            start_rows(dnext_ref, 1 - s)
            reduce_rows(s)

        @pl.when((slot == s) & (i + 1 == n))
        def _():
            wait_rows(s)
            reduce_rows(s)


def _combine_call(dest_tiles, ys, wts_col, xp, mod, g_final, seq, final):
    t, d = xp.shape
    tm = TM_TOK
    nt = t // tm
    tiles_per_seq = seq // tm
    return pl.pallas_call(
        functools.partial(_combine_kernel, final=final),
        grid=(nt,),
        in_specs=[
            pl.BlockSpec((1, 1, TOP_K * tm), lambda i: (jnp.minimum(i + 1, nt - 1), 0, 0),
                         memory_space=pltpu.SMEM),
            pl.BlockSpec((1, 1, TOP_K * tm), lambda i: (0, 0, 0), memory_space=pltpu.SMEM),
            pl.BlockSpec(memory_space=pl.ANY),
            pl.BlockSpec((tm, TOP_K), lambda i: (i, 0)),
            pl.BlockSpec((tm, d), lambda i: (i, 0)),
            pl.BlockSpec((1, N_MOD, d), lambda i: (i // tiles_per_seq, 0, 0)),
            pl.BlockSpec((1, d), lambda i: (0, 0)),
        ],
        out_specs=pl.BlockSpec((tm, d), lambda i: (i, 0)),
        out_shape=jax.ShapeDtypeStruct((t, d), F32),
        scratch_shapes=[
            pltpu.VMEM((TOP_K, tm, ys.shape[1]), ys.dtype),
            pltpu.VMEM((TOP_K, tm, ys.shape[1]), ys.dtype),
            pltpu.SemaphoreType.DMA((2,)),
        ],
        compiler_params=_cparams(("arbitrary",)),
        name="moe_combine",
    )(dest_tiles, dest_tiles, ys, wts_col, xp, mod, g_final)


def _reorder_w_in(w):
    cq, ckv, kpe, gb, gc, u, la, lc = jnp.split(
        w, [512, 768, 832, 1856, 2880, 3904, 5952], axis=1)
    pad = jnp.zeros((w.shape[0], N_PROJ - 8000), w.dtype)
    return jnp.concatenate([la, lc, gb, gc, u, cq, ckv, kpe, pad], axis=1).astype(BF16)


def kernel(x, c, positions, w_ada, b_ada, g_mix, w_in, g_q, w_uq, g_kv, w_ukv, w_o_attn, conv_w,
           w_conv_out, w_out, g_ffn, w_router, b_router, w1_e, w3_e, w2_e, w1_s, w3_s, w2_s, g_final):
    batch, seq, d = x.shape
    depth = w_ada.shape[0]
    t = batch * seq
    xf = x.reshape(t, d)

    c8 = jnp.zeros((8, d), F32).at[:batch].set(c)
    mod_all = _ada_call(c8, w_ada, b_ada.reshape(depth, 1, N_MOD * d))

    inv = 1.0 / (ROPE_THETA ** (jnp.arange(0, QK_ROPE, 2, dtype=F32) / QK_ROPE))
    inv128 = jnp.concatenate([inv, inv, jnp.zeros((LANES - QK_ROPE,), F32)]).reshape(1, LANES)
    cos_t, sin_t = _rope_call(positions.reshape(t, 1), inv128)

    for l in range(depth):
        mod = mod_all[l, :batch].reshape(batch, N_MOD, d)
        wq = jnp.pad(w_uq[l], ((0, 0), (0, 0), (0, HEAD_PAD - QK_HEAD))).reshape(
            Q_LORA, N_HEADS * HEAD_PAD).astype(BF16)
        wkv = w_ukv[l].reshape(KV_LORA, N_HEADS * (QK_NOPE + V_HEAD)).astype(BF16)

        proj = _in_call(xf, g_mix[l].reshape(1, d), mod, _reorder_w_in(w_in[l]), seq)
        q, k, v = _qkv_call(proj, cos_t, sin_t, g_q[l].reshape(1, Q_LORA),
                            g_kv[l].reshape(1, KV_LORA), wq, wkv)
        o = _attn_call(q, k, v, batch, seq)
        xf = _merge_call(o, proj, xf, mod, conv_w[l].reshape(3, CONV_WIDTH),
                         w_o_attn[l].astype(BF16), w_conv_out[l].astype(BF16),
                         w_out[l].astype(BF16), seq)

        hp, idx_t, wts_t, rank_t, cnt = _route_call(
            xf, g_ffn[l].reshape(1, d), mod, w_router[l].T, b_router[l].reshape(N_EXPERTS, 1), seq)
        tile_info, nact, zero_start, dest_tiles, n_rows = _row_plan(idx_t, rank_t, cnt, t)
        xs, xp = _dispatch_call(zero_start, dest_tiles, hp, xf, mod, w1_s[l].astype(BF16),
                                w3_s[l].astype(BF16), w2_s[l].astype(BF16), n_rows, seq)
        ys = _expert_call(l, tile_info, nact, xs, w1_e, w3_e, w2_e)
        xf = _combine_call(dest_tiles, ys, wts_t.T, xp, mod, g_final.reshape(1, d), seq,
                           final=(l == depth - 1))

    return xf.reshape(batch, seq, d)
```
